```python
import math
import jax, jax.numpy as jnp
from jax import lax
import numpy as np

D_MODEL = 1024
BATCH = 8
SEQ = 4096
DEPTH = 2

CTX_LEN = 256
GRID_W = 64
N_MOD = 9
EPS = 1e-6
D_FF = 256 * (-(-(8 * D_MODEL // 3) // 256))

SSD_HEAD_DIM = 64
SSD_HEADS = D_MODEL // SSD_HEAD_DIM
SSD_INNER = SSD_HEADS * SSD_HEAD_DIM
SSD_GROUPS = 2
SSD_STATE = 128
SSD_CONV = 5
SSD_CHUNK = 128
CONV_DIM = SSD_INNER + 2 * SSD_GROUPS * SSD_STATE

POOL_WINDOWS = (2, 4, 8, 16)
N_POOL = len(POOL_WINDOWS)
POOL_WIDTH = D_MODEL
POOL_GROUP = POOL_WIDTH // N_POOL

MIX_A_WIDTH = SSD_INNER + POOL_WIDTH
IN_A = SSD_INNER + CONV_DIM + 2 * SSD_HEADS + POOL_WIDTH

NA_HEAD_DIM = 64
NA_HEADS = D_MODEL // NA_HEAD_DIM
NA_WIDTH = NA_HEADS * NA_HEAD_DIM
WIN_ROWS = 8
WIN_COLS = 16
COL_BLOCK = 16
COL_BAND = COL_BLOCK + WIN_COLS

N_EVEN = (DEPTH + 1) // 2
N_ODD = DEPTH // 2

kernel_name = 'hybrid_ssd_pool_natten_macaron_dit'


def _rmsnorm(h, g):
    hf = h.astype(jnp.float32)
    y = hf * lax.rsqrt(jnp.mean(hf * hf, axis=-1, keepdims=True) + EPS) * g.astype(jnp.float32)
    return y.astype(h.dtype)


def _modnorm(h, g, shift, scale):
    return _rmsnorm(h, g) * (1 + scale) + shift


def _swiglu(h, w1, w3, w2):
    return (jax.nn.silu(h @ w1) * (h @ w3)) @ w2


def _dwconv(u, w, b):
    ch = u.shape[-1]
    out = lax.conv_general_dilated(
        u, w[:, None, :].astype(u.dtype), window_strides=(1,),
        padding=[(SSD_CONV // 2, SSD_CONV // 2)],
        dimension_numbers=('NWC', 'WIO', 'NWC'), feature_group_count=ch)
    return out + b


def _ssd_chunked(xs, dt, a_coef, bm, cm, h0):
    bsz, l, nh, hp = xs.shape
    g = bm.shape[2]
    r = nh // g
    nc = l // SSD_CHUNK
    xc = (xs * dt[..., None]).reshape(bsz, nc, SSD_CHUNK, g, r, hp)
    bc = bm.reshape(bsz, nc, SSD_CHUNK, g, -1)
    cc = cm.reshape(bsz, nc, SSD_CHUNK, g, -1)
    acum = jnp.cumsum((dt * a_coef).reshape(bsz, nc, SSD_CHUNK, g, r), axis=2)
    at = jnp.moveaxis(acum, 2, -1)
    lower = jnp.tril(jnp.ones((SSD_CHUNK, SSD_CHUNK), dtype=bool))
    seg = jnp.exp(jnp.where(lower, at[..., :, None] - at[..., None, :], -jnp.inf))
    cb = jnp.einsum('bclgn,bcsgn->bcgls', cc, bc)
    y_diag = jnp.einsum('bcgls,bcgrls,bcsgrp->bclgrp', cb, seg, xc)
    decay_to_end = jnp.exp(acum[:, :, -1:] - acum)
    states = jnp.einsum('bclgn,bclgr,bclgrp->bcgrpn', bc, decay_to_end, xc)
    chunk_decay = jnp.exp(acum[:, :, -1])

    def step(h, inp):
        s_c, d_c = inp
        return h * d_c[..., None, None] + s_c, h

    h_last, h_prev = lax.scan(step, h0, (jnp.moveaxis(states, 1, 0), jnp.moveaxis(chunk_decay, 1, 0)))
    h_prev = jnp.moveaxis(h_prev, 0, 1)
    y_off = jnp.einsum('bclgn,bcgrpn,bclgr->bclgrp', cc, h_prev, jnp.exp(acum))
    return (y_diag + y_off).reshape(bsz, l, nh, hp), h_last


def _multiscale_pool(up, pool_w, pool_scale):
    bsz, l, _ = up.shape
    u = up.astype(jnp.float32).reshape(bsz, l, N_POOL, POOL_GROUP)
    cs = jnp.concatenate([jnp.zeros((bsz, 1, N_POOL, POOL_GROUP), jnp.float32), jnp.cumsum(u, axis=1)], axis=1)
    t = np.arange(l)[:, None]
    w = np.array(POOL_WINDOWS)[None, :]
    lo = np.clip(t - w // 2, 0, l - 1)
    hi = np.clip(t + w - 1 - w // 2, 0, l - 1)
    gi = np.arange(N_POOL)[None, :]
    win_sum = cs[:, hi + 1, gi] - cs[:, lo, gi]
    mean = win_sum / (hi - lo + 1).astype(np.float32)[None, :, :, None]
    mixed = jnp.einsum('blgc,gcd->blgd', mean - u, pool_w.astype(jnp.float32))
    return (mixed.reshape(bsz, l, POOL_WIDTH) * pool_scale.astype(jnp.float32)).astype(up.dtype)


def _ssd_pool_mixer(hl, hc, in_w, conv_w, conv_b, dt_bias, a_log, d_skip, gn_g, pool_w, pool_scale, out_w, need_ctx_out):
    f32 = jnp.float32
    a_coef = -jnp.exp(a_log.astype(f32))
    dt_b = dt_bias.astype(f32)
    d_h = d_skip.astype(f32)[:, None]
    splits = [SSD_INNER, SSD_INNER + CONV_DIM, SSD_INNER + CONV_DIM + 2 * SSD_HEADS]

    def project(h):
        bsz, l, _ = h.shape
        z, xbc, dt_raw, up = jnp.split(h @ in_w, splits, axis=-1)
        xbc = jax.nn.silu(_dwconv(xbc, conv_w, conv_b)).astype(f32)
        xs, bm, cm = jnp.split(xbc, [SSD_INNER, SSD_INNER + SSD_GROUPS * SSD_STATE], axis=-1)
        xs = xs.reshape(bsz, l, SSD_HEADS, SSD_HEAD_DIM)
        bm = bm.reshape(bsz, l, SSD_GROUPS, SSD_STATE)
        cm = cm.reshape(bsz, l, SSD_GROUPS, SSD_STATE)
        dt = jax.nn.softplus(dt_raw.astype(f32).reshape(bsz, l, 2, SSD_HEADS) + dt_b)
        return z, xs, bm, cm, dt, up

    def scan(seq, direction, h0):
        _, xs, bm, cm, dt, _ = seq
        if direction == 1:
            flip = lambda a: jnp.flip(a, axis=1)
        else:
            flip = lambda a: a
        y, h_last = _ssd_chunked(flip(xs), flip(dt[:, :, direction]), a_coef[direction], flip(bm), flip(cm), h0)
        return flip(y), h_last

    def finish(h, seq, y_f, y_b):
        z, xs, _, _, _, up = seq
        bsz, l, _ = h.shape
        y = (y_f + y_b + d_h * xs).reshape(bsz, l, SSD_INNER) * jax.nn.silu(z.astype(f32))
        y = _rmsnorm(y, gn_g).astype(h.dtype)
        pooled = _multiscale_pool(up, pool_w, pool_scale)
        return jnp.concatenate([y, pooled], axis=-1) @ out_w

    seq_c = project(hc)
    seq_l = project(hl)
    h0 = jnp.zeros((hc.shape[0], SSD_GROUPS, SSD_HEADS // SSD_GROUPS, SSD_HEAD_DIM, SSD_STATE), f32)
    yc_f, hc_f = scan(seq_c, 0, h0)
    yc_b, hc_b = scan(seq_c, 1, h0)
    yl_f, _ = scan(seq_l, 0, hc_f)
    yl_b, _ = scan(seq_l, 1, hc_b)
    out_l = finish(hl, seq_l, yl_f, yl_b)
    out_c = finish(hc, seq_c, yc_f, yc_b) if need_ctx_out else None
    return out_l, out_c


def _na_mixer(hl, hc, qkv_w, rpb, out_w, need_ctx_out):
    f32 = jnp.float32
    bsz, l, _ = hl.shape
    rows = l // GRID_W
    kr = min(WIN_ROWS, rows)
    qscale = NA_HEAD_DIM ** -0.5

    t = (hl @ qkv_w).reshape(bsz, l, 3, NA_HEADS, NA_HEAD_DIM)
    q = (t[:, :, 0] * qscale).reshape(bsz, rows, GRID_W, NA_HEADS, NA_HEAD_DIM)
    k = t[:, :, 1].reshape(bsz, rows, GRID_W, NA_HEADS, NA_HEAD_DIM)
    v = t[:, :, 2].reshape(bsz, rows, GRID_W, NA_HEADS, NA_HEAD_DIM)
    if need_ctx_out:
        tc = (hc @ qkv_w).reshape(hc.shape[0], hc.shape[1], 3, NA_HEADS, NA_HEAD_DIM)
    else:
        tc = (hc @ qkv_w[:, NA_WIDTH:]).reshape(hc.shape[0], hc.shape[1], 2, NA_HEADS, NA_HEAD_DIM)
    kc, vc = tc[:, :, -2], tc[:, :, -1]

    row_start = np.clip(np.arange(rows) - kr // 2, 0, rows - kr)
    row_off = row_start[:, None] + np.arange(kr)[None, :] - np.arange(rows)[:, None] + WIN_ROWS - 1
    col_start = np.clip(np.arange(GRID_W) - WIN_COLS // 2, 0, GRID_W - WIN_COLS)
    n_cb = GRID_W // COL_BLOCK
    band_start = np.minimum(col_start[::COL_BLOCK], GRID_W - COL_BAND)
    band_cols = band_start[:, None] + np.arange(COL_BAND)[None, :]
    qcol = np.arange(GRID_W).reshape(n_cb, COL_BLOCK)
    cs_q = col_start[qcol][:, :, None]
    col_mask = (band_cols[:, None, :] >= cs_q) & (band_cols[:, None, :] < cs_q + WIN_COLS)
    col_idx = np.clip(band_cols[:, None, :] - qcol[:, :, None] + WIN_COLS - 1, 0, 2 * WIN_COLS - 2)
    rpb_cols = rpb.astype(f32)[:, :, col_idx]
    n_win = kr * COL_BAND

    def row_fn(inp):
        q_row, rs, ro = inp
        kb = lax.dynamic_slice_in_dim(k, rs, kr, axis=1)[:, :, band_cols]
        vb = lax.dynamic_slice_in_dim(v, rs, kr, axis=1)[:, :, band_cols]
        qb = q_row.reshape(bsz, n_cb, COL_BLOCK, NA_HEADS, NA_HEAD_DIM)
        s_win = jnp.einsum('bmqhd,bamkhd->bhmqak', qb, kb).astype(f32)
        bias = jnp.transpose(rpb_cols[:, ro], (0, 2, 3, 1, 4))
        s_win = jnp.where(col_mask[:, :, None, :], s_win + bias, -jnp.inf)
        s_ctx = jnp.einsum('bmqhd,bkhd->bhmqk', qb, kc).astype(f32)
        s_all = jnp.concatenate([s_win.reshape(bsz, NA_HEADS, n_cb, COL_BLOCK, n_win), s_ctx], axis=-1)
        p = jax.nn.softmax(s_all, axis=-1).astype(v.dtype)
        p_win = p[..., :n_win].reshape(s_win.shape)
        o = (jnp.einsum('bhmqak,bamkhd->bmqhd', p_win, vb)
             + jnp.einsum('bhmqk,bkhd->bmqhd', p[..., n_win:], vc))
        return o.reshape(bsz, GRID_W, NA_WIDTH)

    o = lax.map(row_fn, (jnp.moveaxis(q, 1, 0), jnp.asarray(row_start, jnp.int32), jnp.asarray(row_off, jnp.int32)))
    out_l = jnp.moveaxis(o, 0, 1).reshape(bsz, l, NA_WIDTH) @ out_w

    out_c = None
    if need_ctx_out:
        qc = tc[:, :, 0] * qscale
        pc = jax.nn.softmax(jnp.einsum('bqhd,bkhd->bhqk', qc, kc).astype(f32), axis=-1).astype(vc.dtype)
        out_c = jnp.einsum('bhqk,bkhd->bqhd', pc, vc).reshape(hc.shape[0], hc.shape[1], NA_WIDTH) @ out_w
    return out_l, out_c


def setup_inputs(seed: int = 0) -> dict:
    key = jax.random.key(seed)
    ks = jax.random.split(key, 26)
    f32 = jnp.float32
    D = D_MODEL

    def nrm(k, shape, scale):
        return jax.random.normal(k, shape, f32) * scale

    dt0 = jnp.exp(jax.random.uniform(ks[14], (N_EVEN, 2, SSD_HEADS), f32, math.log(1e-3), math.log(1e-1)))
    return {
        'x': nrm(ks[0], (BATCH, SEQ, D), 1.0),
        'c': nrm(ks[1], (BATCH, D), 1.0),
        'ctx': nrm(ks[2], (BATCH, CTX_LEN, D), 1.0),
        'c_ctx': nrm(ks[3], (D,), 1.0),
        'ada_w': nrm(ks[4], (DEPTH, D, N_MOD * D), D ** -0.5),
        'ada_b': nrm(ks[5], (DEPTH, N_MOD * D), 0.02),
        'norm_g': 1.0 + nrm(ks[6], (DEPTH, 3, D), 0.1),
        'ffn_w1': nrm(ks[7], (DEPTH, 2, D, D_FF), D ** -0.5),
        'ffn_w3': nrm(ks[8], (DEPTH, 2, D, D_FF), D ** -0.5),
        'ffn_w2': nrm(ks[9], (DEPTH, 2, D_FF, D), D_FF ** -0.5),
        'ssd_in_w': nrm(ks[10], (N_EVEN, D, IN_A), D ** -0.5),
        'ssd_conv_w': nrm(ks[11], (N_EVEN, SSD_CONV, CONV_DIM), SSD_CONV ** -0.5),
        'ssd_conv_b': nrm(ks[12], (N_EVEN, CONV_DIM), 0.02),
        'ssd_dt_bias': dt0 + jnp.log(-jnp.expm1(-dt0)),
        'ssd_a_log': jnp.log(jax.random.uniform(ks[13], (N_EVEN, 2, SSD_HEADS), f32, 1.0, 16.0)),
        'ssd_d': 1.0 + nrm(ks[15], (N_EVEN, SSD_HEADS), 0.1),
        'ssd_norm_g': 1.0 + nrm(ks[16], (N_EVEN, SSD_INNER), 0.1),
        'pool_w': nrm(ks[17], (N_EVEN, N_POOL, POOL_GROUP, POOL_GROUP), POOL_GROUP ** -0.5),
        'pool_scale': 1.0 + nrm(ks[18], (N_EVEN, POOL_WIDTH), 0.1),
        'mix_a_out_w': nrm(ks[19], (N_EVEN, MIX_A_WIDTH, D), MIX_A_WIDTH ** -0.5),
        'na_qkv_w': nrm(ks[20], (N_ODD, D, 3 * NA_WIDTH), D ** -0.5),
        'na_rpb': nrm(ks[21], (N_ODD, NA_HEADS, 2 * WIN_ROWS - 1, 2 * WIN_COLS - 1), 0.2),
        'na_out_w': nrm(ks[22], (N_ODD, NA_WIDTH, D), NA_WIDTH ** -0.5),
        'final_norm_g': 1.0 + nrm(ks[23], (D,), 0.1),
    }


def reference(x, c, ctx, c_ctx, ada_w, ada_b, norm_g, ffn_w1, ffn_w3, ffn_w2, ssd_in_w, ssd_conv_w, ssd_conv_b,
              ssd_dt_bias, ssd_a_log, ssd_d, ssd_norm_g, pool_w, pool_scale, mix_a_out_w, na_qkv_w, na_rpb,
              na_out_w, final_norm_g):
    bsz = x.shape[0]
    xl, xc = x, ctx
    s_lat = jax.nn.silu(c)
    s_ctx = jax.nn.silu(c_ctx)
    for i in range(DEPTH):
        mod_l = (s_lat @ ada_w[i] + ada_b[i]).reshape(bsz, 1, N_MOD, D_MODEL)
        mod_c = (s_ctx @ ada_w[i] + ada_b[i]).reshape(N_MOD, D_MODEL)
        ml = [mod_l[:, :, j] for j in range(N_MOD)]
        mc = [mod_c[j] for j in range(N_MOD)]
        need_ctx = i < DEPTH - 1
        xl = xl + 0.5 * ml[2] * _swiglu(_modnorm(xl, norm_g[i, 0], ml[0], ml[1]), ffn_w1[i, 0], ffn_w3[i, 0], ffn_w2[i, 0])
        xc = xc + 0.5 * mc[2] * _swiglu(_modnorm(xc, norm_g[i, 0], mc[0], mc[1]), ffn_w1[i, 0], ffn_w3[i, 0], ffn_w2[i, 0])
        hl = _modnorm(xl, norm_g[i, 1], ml[3], ml[4])
        hc = _modnorm(xc, norm_g[i, 1], mc[3], mc[4])
        j = i // 2
        if i % 2 == 0:
            yl, yc = _ssd_pool_mixer(hl, hc, ssd_in_w[j], ssd_conv_w[j], ssd_conv_b[j], ssd_dt_bias[j], ssd_a_log[j],
                                     ssd_d[j], ssd_norm_g[j], pool_w[j], pool_scale[j], mix_a_out_w[j], need_ctx)
        else:
            yl, yc = _na_mixer(hl, hc, na_qkv_w[j], na_rpb[j], na_out_w[j], need_ctx)
        xl = xl + ml[5] * yl
        xl = xl + 0.5 * ml[8] * _swiglu(_modnorm(xl, norm_g[i, 2], ml[6], ml[7]), ffn_w1[i, 1], ffn_w3[i, 1], ffn_w2[i, 1])
        if need_ctx:
            xc = xc + mc[5] * yc
            xc = xc + 0.5 * mc[8] * _swiglu(_modnorm(xc, norm_g[i, 2], mc[6], mc[7]), ffn_w1[i, 1], ffn_w3[i, 1], ffn_w2[i, 1])
    return _rmsnorm(xl, final_norm_g)
```

```python
import functools

import numpy as np
import jax
import jax.numpy as jnp
from jax import lax
from jax.experimental import pallas as pl
from jax.experimental.pallas import tpu as pltpu

F32 = jnp.float32
BF16 = jnp.bfloat16

D = 1024
BATCH = 8
SEQ = 4096
CTX = 256
N_MOD = 9
EPS = 1e-6
D_FF = 2816
T_LAT = BATCH * SEQ
T_CTX = BATCH * CTX
N_TOK = T_LAT + T_CTX
MOD_ROWS = 16
CTX_MOD_ROW = BATCH

HEADS = 16
HEAD_DIM = 64
SSD_STATE = 128
SSD_CONV = 5
CHUNK = 128
CONV_DIM = 1536
N_POOL = 4
POOL_GROUP = 256
POOL_WINDOWS = (2, 4, 8, 16)

GRID_W = 64
GRID_ROWS = SEQ // GRID_W
WIN_ROWS = 8
WIN_COLS = 16

HALO = 8
SEQ_TILE = 256
VMEM_LIMIT = 56 * 1024 * 1024


def _cparams(n_axes):
    return pltpu.CompilerParams(
        dimension_semantics=("arbitrary",) * n_axes, vmem_limit_bytes=VMEM_LIMIT)


def _dot(a, b):
    return jnp.dot(a, b, preferred_element_type=F32)


def _dot_nt(a, b):
    return lax.dot_general(a, b, (((1,), (1,)), ((), ())), preferred_element_type=F32)


def _split3(x):
    hi = x.astype(BF16)
    r = x - hi.astype(F32)
    mid = r.astype(BF16)
    lo = (r - mid.astype(F32)).astype(BF16)
    return hi, mid, lo


def _rms_mod(x, g, shift, scale):
    ms = jnp.mean(x * x, axis=-1, keepdims=True)
    return (x * lax.rsqrt(ms + EPS) * g) * (1.0 + scale) + shift


def _silu(x):
    return x * (1.0 / (1.0 + jnp.exp(-x)))


def _softplus(x):
    return jnp.maximum(x, 0.0) + jnp.log1p(jnp.exp(-jnp.abs(x)))


def _mod_row(tile_rows):
    lat_tiles = T_LAT // tile_rows
    per_batch = SEQ // tile_rows

    def f(i):
        return jnp.where(i < lat_tiles, i // per_batch, CTX_MOD_ROW)

    return f


def _const_spec(shape):
    nd = len(shape)
    return pl.BlockSpec(shape, lambda *_: (0,) * nd, pipeline_mode=pl.Buffered(1))


def _ada_kernel(c_ref, w_ref, b_ref, o_ref):
    s = _silu(c_ref[...])
    s_hi = s.astype(BF16)
    s_lo = (s - s_hi.astype(F32)).astype(BF16)
    w = w_ref[0]
    w_hi = w.astype(BF16)
    w_lo = (w - w_hi.astype(F32)).astype(BF16)
    o_ref[0] = _dot(s_hi, w_hi) + _dot(s_hi, w_lo) + _dot(s_lo, w_hi) + b_ref[0]


def _ada(c_all, ada_w, ada_b):
    tn = 1024
    depth = ada_w.shape[0]
    n = ada_w.shape[2]
    return pl.pallas_call(
        _ada_kernel,
        grid=(depth, n // tn),
        in_specs=[
            pl.BlockSpec((MOD_ROWS, D), lambda l, j: (0, 0)),
            pl.BlockSpec((1, D, tn), lambda l, j: (l, 0, j)),
            pl.BlockSpec((1, 1, tn), lambda l, j: (l, 0, j)),
        ],
        out_specs=pl.BlockSpec((1, MOD_ROWS, tn), lambda l, j: (l, 0, j)),
        out_shape=jax.ShapeDtypeStruct((depth, MOD_ROWS, n), F32),
        compiler_params=_cparams(2),
        name="ada_mod",
    )(c_all, ada_w, ada_b.reshape(depth, 1, n))


def _ffn_kernel(x_ref, mod_ref, g_ref, w1_ref, w3_ref, w2_ref, *rest, j0, final_norm):
    if final_norm:
        fg_ref, o_ref = rest
    else:
        (o_ref,) = rest
    x = x_ref[...]
    shift = mod_ref[0, j0:j0 + 1, :]
    scale = mod_ref[0, j0 + 1:j0 + 2, :]
    gate = mod_ref[0, j0 + 2:j0 + 3, :]
    h = _rms_mod(x, g_ref[...], shift, scale).astype(BF16)
    h1 = _dot(h, w1_ref[...])
    h3 = _dot(h, w3_ref[...])
    a = (_silu(h1) * h3).astype(BF16)
    y = _dot(a, w2_ref[...])
    out = x + (0.5 * gate) * y
    if final_norm:
        ms = jnp.mean(out * out, axis=-1, keepdims=True)
        out = out * lax.rsqrt(ms + EPS) * fg_ref[...]
    o_ref[...] = out


def _ffn(x, mod, j0, g, w1, w3, w2, *, n_rows, final_g=None, tm=512):
    final_norm = final_g is not None
    in_specs = [
        pl.BlockSpec((tm, D), lambda i: (i, 0)),
        pl.BlockSpec((1, N_MOD, D), lambda i, f=_mod_row(tm): (f(i), 0, 0)),
        _const_spec((1, D)),
        _const_spec((D, D_FF)),
        _const_spec((D, D_FF)),
        _const_spec((D_FF, D)),
    ]
    args = [x, mod, g.reshape(1, D), w1, w3, w2]
    if final_norm:
        in_specs.append(_const_spec((1, D)))
        args.append(final_g.reshape(1, D))
    return pl.pallas_call(
        functools.partial(_ffn_kernel, j0=j0, final_norm=final_norm),
        grid=(n_rows // tm,),
        in_specs=in_specs,
        out_specs=pl.BlockSpec((tm, D), lambda i: (i, 0)),
        out_shape=jax.ShapeDtypeStruct((n_rows, D), F32),
        compiler_params=_cparams(1),
        name="ffn_final" if final_norm else "ffn",
    )(*args)


def _inproj_kernel(x_ref, mod_ref, g_ref, wz_ref, wx_ref, wu_ref, wd0_ref, wd1_ref, wdt_ref,
                   dbr_ref, dbc_ref, z_ref, xbc_ref, up_ref, dtc_ref, dtt_ref):
    x = x_ref[...]
    h = _rms_mod(x, g_ref[...], mod_ref[0, 3:4, :], mod_ref[0, 4:5, :]).astype(BF16)
    z_ref[...] = _dot(h, wz_ref[...])
    xbc_ref[...] = _dot(h, wx_ref[...])
    up_ref[...] = _dot(h, wu_ref[...])
    dtc_ref[0] = _softplus(_dot(h, wd0_ref[...]) + dbr_ref[0])
    dtc_ref[1] = _softplus(_dot(h, wd1_ref[...]) + dbr_ref[1])
    dtt_ref[...] = _softplus(_dot_nt(wdt_ref[...], h) + dbc_ref[...])


def _inproj(x, mod, g, wz, wx, wu, wd0, wd1, wdt, db_row, db_col, tm=512):
    n = x.shape[0]
    row = lambda i: (i, 0)
    return pl.pallas_call(
        _inproj_kernel,
        grid=(n // tm,),
        in_specs=[
            pl.BlockSpec((tm, D), row),
            pl.BlockSpec((1, N_MOD, D), lambda i, f=_mod_row(tm): (f(i), 0, 0)),
            _const_spec((1, D)),
            _const_spec((D, D)),
            _const_spec((D, CONV_DIM)),
            _const_spec((D, D)),
            _const_spec((D, HEADS)),
            _const_spec((D, HEADS)),
            _const_spec((2 * HEADS, D)),
            _const_spec((2, 1, HEADS)),
            _const_spec((2 * HEADS, 1)),
        ],
        out_specs=[
            pl.BlockSpec((tm, D), row),
            pl.BlockSpec((tm, CONV_DIM), row),
            pl.BlockSpec((tm, D), row),
            pl.BlockSpec((2, tm, HEADS), lambda i: (0, i, 0)),
            pl.BlockSpec((2 * HEADS, tm), lambda i: (0, i)),
        ],
        out_shape=[
            jax.ShapeDtypeStruct((n, D), F32),
            jax.ShapeDtypeStruct((n, CONV_DIM), F32),
            jax.ShapeDtypeStruct((n, D), F32),
            jax.ShapeDtypeStruct((2, n, HEADS), F32),
            jax.ShapeDtypeStruct((2 * HEADS, n), F32),
        ],
        compiler_params=_cparams(1),
        name="ssd_inproj",
    )(x, mod, g.reshape(1, D), wz, wx, wu, wd0, wd1, wdt, db_row, db_col)


def _seq_tile_info(i):
    lat_tiles = T_LAT // SEQ_TILE
    per_seq = SEQ // SEQ_TILE
    is_lat = i < lat_tiles
    pos = jnp.where(is_lat, i % per_seq, 0)
    first = pos == 0
    last = jnp.where(is_lat, pos == per_seq - 1, True)
    seq_len = jnp.where(is_lat, SEQ, CTX)
    return first, last, pos * SEQ_TILE, seq_len


def _halo_specs(width):
    blocks_per_tile = SEQ_TILE // HALO
    n_blocks = N_TOK // HALO
    prev = pl.BlockSpec((HALO, width), lambda i: (jnp.maximum(i * blocks_per_tile - 1, 0), 0))
    nxt = pl.BlockSpec(
        (HALO, width), lambda i: (jnp.minimum((i + 1) * blocks_per_tile, n_blocks - 1), 0))
    return prev, nxt


def _with_halo(prev_ref, cur_ref, next_ref, first, last):
    prev = jnp.where(first, 0.0, prev_ref[...])
    nxt = jnp.where(last, 0.0, next_ref[...])
    return jnp.concatenate([prev, cur_ref[...], nxt], axis=0)


def _conv_kernel(prev_ref, cur_ref, next_ref, w_ref, b_ref, xs_ref, bm_ref, cm_ref):
    first, last, _, _ = _seq_tile_info(pl.program_id(0))
    ext = _with_halo(prev_ref, cur_ref, next_ref, first, last)
    acc = b_ref[...] + w_ref[0:1, :] * ext[HALO - 2:HALO - 2 + SEQ_TILE]
    for k in range(1, SSD_CONV):
        acc = acc + w_ref[k:k + 1, :] * ext[HALO - 2 + k:HALO - 2 + k + SEQ_TILE]
    act = _silu(acc)
    xs_ref[...] = act[:, :D]
    bm_ref[...] = act[:, D:D + 2 * SSD_STATE]
    cm_ref[...] = act[:, D + 2 * SSD_STATE:]


def _conv(xbc, conv_w, conv_b):
    n = xbc.shape[0]
    prev, nxt = _halo_specs(CONV_DIM)
    row = lambda i: (i, 0)
    return pl.pallas_call(
        _conv_kernel,
        grid=(n // SEQ_TILE,),
        in_specs=[prev, pl.BlockSpec((SEQ_TILE, CONV_DIM), row), nxt,
                  _const_spec((SSD_CONV, CONV_DIM)), _const_spec((1, CONV_DIM))],
        out_specs=[pl.BlockSpec((SEQ_TILE, D), row),
                   pl.BlockSpec((SEQ_TILE, 2 * SSD_STATE), row),
                   pl.BlockSpec((SEQ_TILE, 2 * SSD_STATE), row)],
        out_shape=[jax.ShapeDtypeStruct((n, D), F32),
                   jax.ShapeDtypeStruct((n, 2 * SSD_STATE), F32),
                   jax.ShapeDtypeStruct((n, 2 * SSD_STATE), F32)],
        compiler_params=_cparams(1),
        name="ssd_conv",
    )(xbc, xbc, xbc, conv_w, conv_b.reshape(1, CONV_DIM))


def _ssd_kernel(xs_ref, bm_ref, cm_ref, dtc_ref, dtt_ref, alc_ref, alr_ref, y_ref, state_ref):
    d = pl.program_id(1)
    k = pl.program_id(2)

    @pl.when(k == 0)
    def _():
        state_ref[...] = jnp.zeros_like(state_ref)

    fwd = d == 0
    xs_bf = xs_ref[...].astype(BF16)
    bm = bm_ref[...]
    cm = cm_ref[...]
    dtc = dtc_ref[0]
    dtt = dtt_ref[...]
    a_t = dtt * (-jnp.exp(alc_ref[0]))
    a_c = dtc * (-jnp.exp(alr_ref[0]))

    ri = lax.broadcasted_iota(jnp.int32, (CHUNK, CHUNK), 0)
    ci = lax.broadcasted_iota(jnp.int32, (CHUNK, CHUNK), 1)
    sgn = jnp.where(fwd, 1, -1)
    causal = (ci - ri) * sgn <= 0
    tri_ls = causal.astype(BF16)
    tri_ts = ((ri - ci) * sgn <= 0).astype(BF16)

    t_hi, t_mid, t_lo = _split3(a_t)
    cum_t = _dot(t_hi, tri_ts) + _dot(t_mid, tri_ts) + _dot(t_lo, tri_ts)
    c_hi, c_mid, c_lo = _split3(a_c)
    cum_c = _dot(tri_ls, c_hi) + _dot(tri_ls, c_mid) + _dot(tri_ls, c_lo)
    tot_c = jnp.sum(a_t, axis=1, keepdims=True)
    tot_r = jnp.sum(a_c, axis=0, keepdims=True)

    w_t = jnp.exp(tot_c - cum_t) * dtt

    eh = lax.broadcasted_iota(jnp.int32, (HEADS, D), 0)
    ec = lax.broadcasted_iota(jnp.int32, (HEADS, D), 1)
    expand = ((ec >= eh * HEAD_DIM) & (ec < (eh + 1) * HEAD_DIM)).astype(BF16)
    ex = jnp.exp(cum_c)
    ex_hi = ex.astype(BF16)
    ex_lo = (ex - ex_hi.astype(F32)).astype(BF16)
    scale_off = _dot(ex_hi, expand) + _dot(ex_lo, expand)
    cd = jnp.broadcast_to(jnp.exp(tot_r), (8, HEADS))
    cd_hi = cd.astype(BF16)
    cd_lo = (cd - cd_hi.astype(F32)).astype(BF16)
    chunk_decay = (_dot(cd_hi, expand) + _dot(cd_lo, expand))[0:1]

    state = state_ref[...]
    state_bf = state.astype(BF16)
    lane = lax.broadcasted_iota(jnp.int32, (CHUNK, 2 * HEAD_DIM), 1)
    low_half = lane < HEAD_DIM
    neg_inf = jnp.float32(-jnp.inf)
    group_w = (HEADS // 2) * HEAD_DIM

    for g in range(2):
        bg = bm[:, g * SSD_STATE:(g + 1) * SSD_STATE]
        cg_bf = cm[:, g * SSD_STATE:(g + 1) * SSD_STATE].astype(BF16)
        cb = _dot_nt(cg_bf, bg.astype(BF16))
        bg_t = bg.T
        y_off = _dot(cg_bf, state_bf[:, g * group_w:(g + 1) * group_w])
        for pr in range(HEADS // 4):
            h0 = g * (HEADS // 2) + 2 * pr
            lo = h0 * HEAD_DIM
            sl = slice(lo, lo + 2 * HEAD_DIM)
            xp = xs_bf[:, sl]
            zero = jnp.zeros_like(xp)
            rhs = jnp.concatenate(
                [jnp.where(low_half, xp, zero), jnp.where(low_half, zero, xp)], axis=0)

            def intra(h):
                diff = cum_c[:, h:h + 1] - cum_t[h:h + 1, :]
                seg = jnp.exp(jnp.where(causal, diff, neg_inf))
                return (cb * seg * dtt[h:h + 1, :]).astype(BF16)

            def to_state(h):
                return (bg_t * w_t[h:h + 1, :]).astype(BF16)

            y_diag = _dot(jnp.concatenate([intra(h0), intra(h0 + 1)], axis=1), rhs)
            s_new = _dot(jnp.concatenate([to_state(h0), to_state(h0 + 1)], axis=1), rhs)
            lo_g = lo - g * group_w
            y_ref[0, :, sl] = y_diag + y_off[:, lo_g:lo_g + 2 * HEAD_DIM] * scale_off[:, sl]
            state_ref[:, sl] = state[:, sl] * chunk_decay[:, sl] + s_new


def _ssd_chunk_index(b, d, k):
    lat_chunks = SEQ // CHUNK
    ctx_chunks = CTX // CHUNK
    pos = jnp.where(d == 0, k,
                    jnp.where(k < ctx_chunks, ctx_chunks - 1 - k, 2 * ctx_chunks + lat_chunks - 1 - k))
    return jnp.where(pos < ctx_chunks,
                     T_LAT // CHUNK + ctx_chunks * b + pos,
                     lat_chunks * b + pos - ctx_chunks)


def _ssd(xs, bm, cm, dtc, dtt, a_log):
    n = xs.shape[0]
    steps = (SEQ + CTX) // CHUNK
    cidx = _ssd_chunk_index
    return pl.pallas_call(
        _ssd_kernel,
        grid=(BATCH, 2, steps),
        in_specs=[
            pl.BlockSpec((CHUNK, D), lambda b, d, k: (cidx(b, d, k), 0)),
            pl.BlockSpec((CHUNK, 2 * SSD_STATE), lambda b, d, k: (cidx(b, d, k), 0)),
            pl.BlockSpec((CHUNK, 2 * SSD_STATE), lambda b, d, k: (cidx(b, d, k), 0)),
            pl.BlockSpec((1, CHUNK, HEADS), lambda b, d, k: (d, cidx(b, d, k), 0)),
            pl.BlockSpec((HEADS, CHUNK), lambda b, d, k: (d, cidx(b, d, k))),
            pl.BlockSpec((1, HEADS, 1), lambda b, d, k: (d, 0, 0)),
            pl.BlockSpec((1, 1, HEADS), lambda b, d, k: (d, 0, 0)),
        ],
        out_specs=pl.BlockSpec((1, CHUNK, D), lambda b, d, k: (d, cidx(b, d, k), 0)),
        out_shape=jax.ShapeDtypeStruct((2, n, D), F32),
        scratch_shapes=[pltpu.VMEM((SSD_STATE, D), F32)],
        compiler_params=_cparams(3),
        name="ssd_scan",
    )(xs, bm, cm, dtc, dtt, a_log.reshape(2, HEADS, 1), a_log.reshape(2, 1, HEADS))


def _window_sum(e, w):
    acc = e
    span = 1
    while span < w:
        n = acc.shape[0] - span
        acc = acc[:n] + acc[span:span + n]
        span *= 2
    start = HALO - w // 2
    return acc[start:start + SEQ_TILE]


def _finish_kernel(x_ref, mod_ref, y_ref, xs_ref, z_ref, uprev_ref, up_ref, unext_ref,
                   dsk_ref, gn_ref, pw_ref, ps_ref, ow_ref, o_ref):
    first, last, row0, seq_len = _seq_tile_info(pl.program_id(0))
    y = (y_ref[0] + y_ref[1] + dsk_ref[...] * xs_ref[...]) * _silu(z_ref[...])
    ms = jnp.mean(y * y, axis=-1, keepdims=True)
    yn = (y * lax.rsqrt(ms + EPS) * gn_ref[...]).astype(BF16)
    out = _dot(yn, ow_ref[0:D, :])
    ext = _with_halo(uprev_ref, up_ref, unext_ref, first, last)
    pos = row0 + lax.broadcasted_iota(jnp.int32, (SEQ_TILE, POOL_GROUP), 0)
    for g, w in enumerate(POOL_WINDOWS):
        sl = slice(g * POOL_GROUP, (g + 1) * POOL_GROUP)
        lo = jnp.maximum(pos - w // 2, 0)
        hi = jnp.minimum(pos + (w - 1 - w // 2), seq_len - 1)
        cnt = (hi - lo + 1).astype(F32)
        eg = ext[:, sl]
        mean = _window_sum(eg, w) / cnt
        m = (mean - eg[HALO:HALO + SEQ_TILE]).astype(BF16)
        pooled = (_dot(m, pw_ref[g]) * ps_ref[:, sl]).astype(BF16)
        out = out + _dot(pooled, ow_ref[D + g * POOL_GROUP:D + (g + 1) * POOL_GROUP, :])
    o_ref[...] = x_ref[...] + mod_ref[0, 5:6, :] * out


def _finish(x, mod, y, xs, z, up, d_skip, gn_g, pool_w, pool_scale, out_w):
    n = x.shape[0]
    prev, nxt = _halo_specs(D)
    row = lambda i: (i, 0)
    tile = pl.BlockSpec((SEQ_TILE, D), row)
    return pl.pallas_call(
        _finish_kernel,
        grid=(n // SEQ_TILE,),
        in_specs=[
            tile,
            pl.BlockSpec((1, N_MOD, D), lambda i, f=_mod_row(SEQ_TILE): (f(i), 0, 0)),
            pl.BlockSpec((2, SEQ_TILE, D), lambda i: (0, i, 0)),
            tile, tile, prev, tile, nxt,
            _const_spec((1, D)), _const_spec((1, D)),
            _const_spec((N_POOL, POOL_GROUP, POOL_GROUP)), _const_spec((1, D)),
            _const_spec((2 * D, D)),
        ],
        out_specs=tile,
        out_shape=jax.ShapeDtypeStruct((n, D), F32),
        compiler_params=_cparams(1),
        name="ssd_finish",
    )(x, mod, y, xs, z, up, up, up, d_skip, gn_g.reshape(1, D), pool_w,
      pool_scale.reshape(1, D), out_w)


def _qkv_kernel(x_ref, mod_ref, g_ref, w_ref, q_ref, k_ref, v_ref):
    h = _rms_mod(x_ref[...], g_ref[...], mod_ref[0, 3:4, :], mod_ref[0, 4:5, :]).astype(BF16)
    t = _dot(h, w_ref[...])
    q_ref[...] = (t[:, :D] * (HEAD_DIM ** -0.5)).astype(BF16)
    k_ref[...] = t[:, D:2 * D].astype(BF16)
    v_ref[...] = t[:, 2 * D:].astype(BF16)


def _qkv(x, mod, g, w, tm=512):
    n = x.shape[0]
    row = lambda i: (i, 0)
    out = jax.ShapeDtypeStruct((n, D), BF16)
    return pl.pallas_call(
        _qkv_kernel,
        grid=(n // tm,),
        in_specs=[pl.BlockSpec((tm, D), row),
                  pl.BlockSpec((1, N_MOD, D), lambda i, f=_mod_row(tm): (f(i), 0, 0)),
                  _const_spec((1, D)), _const_spec((D, 3 * D))],
        out_specs=[pl.BlockSpec((tm, D), row)] * 3,
        out_shape=[out, out, out],
        compiler_params=_cparams(1),
        name="na_qkv",
    )(x, mod, g.reshape(1, D), w)


NA_ROWS = 8
NA_KEYS = WIN_ROWS * GRID_W


def _na_kernel(q_ref, k_ref, v_ref, kc_ref, vc_ref, bias_ref, o_ref):
    rb = pl.program_id(2)
    lane = lax.broadcasted_iota(jnp.int32, (GRID_W, 2 * HEAD_DIM), 1)
    low_half = lane < HEAD_DIM
    kc = kc_ref[...]
    vc = vc_ref[...]
    for qi in range(NA_ROWS):
        i = rb * NA_ROWS + qi
        rs = jnp.clip(i - WIN_ROWS // 2, 0, GRID_ROWS - WIN_ROWS)
        kind = i - rs
        start = pl.multiple_of(rs * GRID_W, GRID_W)
        q_row = q_ref[qi * GRID_W:(qi + 1) * GRID_W, :]
        k_win = k_ref[pl.ds(start, NA_KEYS), :]
        v_win = v_ref[pl.ds(start, NA_KEYS), :]
        zero = jnp.zeros_like(q_row)
        outs = []
        for hh in range(2):
            qm = jnp.where(low_half, q_row, zero) if hh == 0 else jnp.where(low_half, zero, q_row)
            s_win = _dot_nt(qm, k_win) + bias_ref[kind, hh]
            s_ctx = _dot_nt(qm, kc)
            m = jnp.maximum(jnp.max(s_win, axis=1, keepdims=True),
                            jnp.max(s_ctx, axis=1, keepdims=True))
            p_win = jnp.exp(s_win - m)
            p_ctx = jnp.exp(s_ctx - m)
            den = jnp.sum(p_win, axis=1, keepdims=True) + jnp.sum(p_ctx, axis=1, keepdims=True)
            o = _dot(p_win.astype(BF16), v_win) + _dot(p_ctx.astype(BF16), vc)
            outs.append(o / den)
        o_ref[qi * GRID_W:(qi + 1) * GRID_W, :] = jnp.where(low_half, outs[0], outs[1]).astype(BF16)


def _na(q, k, v, bias):
    rows = NA_ROWS * GRID_W
    rblocks = GRID_ROWS // NA_ROWS
    pair = 2 * HEAD_DIM
    ctx_blk0 = T_LAT // CTX
    return pl.pallas_call(
        _na_kernel,
        grid=(HEADS // 2, BATCH, rblocks),
        in_specs=[
            pl.BlockSpec((rows, pair), lambda hp, b, r: (b * rblocks + r, hp)),
            pl.BlockSpec((SEQ, pair), lambda hp, b, r: (b, hp)),
            pl.BlockSpec((SEQ, pair), lambda hp, b, r: (b, hp)),
            pl.BlockSpec((CTX, pair), lambda hp, b, r: (ctx_blk0 + b, hp)),
            pl.BlockSpec((CTX, pair), lambda hp, b, r: (ctx_blk0 + b, hp)),
            pl.BlockSpec((WIN_ROWS, 2, GRID_W, NA_KEYS), lambda hp, b, r: (0, hp, 0, 0)),
        ],
        out_specs=pl.BlockSpec((rows, pair), lambda hp, b, r: (b * rblocks + r, hp)),
        out_shape=jax.ShapeDtypeStruct((T_LAT, D), BF16),
        compiler_params=_cparams(3),
        name="na_attn",
    )(q, k, v, k, v, bias)


def _na_bias_table(rpb):
    j = np.arange(GRID_W)[:, None]
    kc = np.arange(GRID_W)[None, :]
    cs = np.clip(j - WIN_COLS // 2, 0, GRID_W - WIN_COLS)
    col_ok = (kc >= cs) & (kc < cs + WIN_COLS)
    cidx = np.clip(kc - j + WIN_COLS - 1, 0, 2 * WIN_COLS - 2)
    ridx = np.arange(WIN_ROWS)[None, :] - np.arange(WIN_ROWS)[:, None] + WIN_ROWS - 1
    t = rpb.astype(F32)[:, ridx[:, :, None, None], cidx[None, None, :, :]]
    t = jnp.where(col_ok[None, None, None], t, -jnp.inf)
    return jnp.transpose(t, (1, 0, 3, 2, 4)).reshape(WIN_ROWS, HEADS, GRID_W, NA_KEYS)


def _linres_kernel(x_ref, mod_ref, a_ref, w_ref, o_ref):
    o_ref[...] = x_ref[...] + mod_ref[0, 5:6, :] * _dot(a_ref[...], w_ref[...])


def _linres(x, mod, a, w, *, n_rows, tm=512):
    row = lambda i: (i, 0)
    return pl.pallas_call(
        _linres_kernel,
        grid=(n_rows // tm,),
        in_specs=[pl.BlockSpec((tm, D), row),
                  pl.BlockSpec((1, N_MOD, D), lambda i, f=_mod_row(tm): (f(i), 0, 0)),
                  pl.BlockSpec((tm, D), row), _const_spec((D, D))],
        out_specs=pl.BlockSpec((tm, D), row),
        out_shape=jax.ShapeDtypeStruct((n_rows, D), F32),
        compiler_params=_cparams(1),
        name="na_out",
    )(x, mod, a, w)


def kernel(x, c, ctx, c_ctx, ada_w, ada_b, norm_g, ffn_w1, ffn_w3, ffn_w2, ssd_in_w, ssd_conv_w,
           ssd_conv_b, ssd_dt_bias, ssd_a_log, ssd_d, ssd_norm_g, pool_w, pool_scale, mix_a_out_w,
           na_qkv_w, na_rpb, na_out_w, final_norm_g):
    xf = jnp.concatenate([x.reshape(T_LAT, D), ctx.reshape(T_CTX, D)], axis=0)
    c_all = jnp.concatenate(
        [c, c_ctx[None, :], jnp.zeros((MOD_ROWS - BATCH - 1, D), F32)], axis=0)
    mod = _ada(c_all, ada_w, ada_b).reshape(2, MOD_ROWS, N_MOD, D)
    w1 = ffn_w1.astype(BF16)
    w3 = ffn_w3.astype(BF16)
    w2 = ffn_w2.astype(BF16)

    m0 = mod[0]
    xf = _ffn(xf, m0, 0, norm_g[0, 0], w1[0, 0], w3[0, 0], w2[0, 0], n_rows=N_TOK)
    in_w = ssd_in_w[0]
    o1, o2, o3 = D, D + CONV_DIM, D + CONV_DIM + 2 * HEADS
    w_dt = in_w[:, o2:o3]
    z, xbc, up, dtc, dtt = _inproj(
        xf, m0, norm_g[0, 1],
        in_w[:, :o1].astype(BF16), in_w[:, o1:o2].astype(BF16), in_w[:, o3:].astype(BF16),
        w_dt[:, :HEADS].astype(BF16), w_dt[:, HEADS:].astype(BF16), w_dt.T.astype(BF16),
        ssd_dt_bias[0].reshape(2, 1, HEADS), ssd_dt_bias[0].reshape(2 * HEADS, 1))
    xs, bm, cm = _conv(xbc, ssd_conv_w[0], ssd_conv_b[0])
    y = _ssd(xs, bm, cm, dtc, dtt, ssd_a_log[0])
    d_skip = jnp.repeat(ssd_d[0].astype(F32), HEAD_DIM).reshape(1, D)
    xf = _finish(xf, m0, y, xs, z, up, d_skip, ssd_norm_g[0], pool_w[0].astype(BF16),
                 pool_scale[0], mix_a_out_w[0].astype(BF16))
    xf = _ffn(xf, m0, 6, norm_g[0, 2], w1[0, 1], w3[0, 1], w2[0, 1], n_rows=N_TOK)

    m1 = mod[1]
    xf = _ffn(xf, m1, 0, norm_g[1, 0], w1[1, 0], w3[1, 0], w2[1, 0], n_rows=N_TOK)
    q, k, v = _qkv(xf, m1, norm_g[1, 1], na_qkv_w[0].astype(BF16))
    att = _na(q, k, v, _na_bias_table(na_rpb[0]))
    xl = _linres(xf, m1, att, na_out_w[0].astype(BF16), n_rows=T_LAT)
    xl = _ffn(xl, m1, 6, norm_g[1, 2], w1[1, 1], w3[1, 1], w2[1, 1], n_rows=T_LAT,
              final_g=final_norm_g)
    return xl.reshape(BATCH, SEQ, D)
```

```python
import functools

import numpy as np
import jax
import jax.numpy as jnp
from jax import lax
from jax.experimental import pallas as pl
from jax.experimental.pallas import tpu as pltpu

F32 = jnp.float32
BF16 = jnp.bfloat16

D = 1024
BATCH = 8
SEQ = 4096
CTX = 256
N_MOD = 9
EPS = 1e-6
D_FF = 2816
T_LAT = BATCH * SEQ
T_CTX = BATCH * CTX
N_TOK = T_LAT + T_CTX
MOD_ROWS = 16
CTX_MOD_ROW = BATCH

HEADS = 16
HEAD_DIM = 64
SSD_STATE = 128
SSD_CONV = 5
CHUNK = 128
CONV_DIM = 1536
CONV_COLS = 256
N_POOL = 4
POOL_GROUP = 256
POOL_WINDOWS = (2, 4, 8, 16)

GRID_W = 64
GRID_ROWS = SEQ // GRID_W
WIN_ROWS = 8
WIN_COLS = 16

HALO = 16
SEQ_TILE = 256
VMEM_LIMIT = 56 * 1024 * 1024
LOG2E = float(np.log2(np.e))


def _cparams(n_axes):
    return pltpu.CompilerParams(
        dimension_semantics=("arbitrary",) * n_axes, vmem_limit_bytes=VMEM_LIMIT)


def _dot(a, b):
    return jnp.dot(a, b, preferred_element_type=F32)


def _dot_nt(a, b):
    return lax.dot_general(a, b, (((1,), (1,)), ((), ())), preferred_element_type=F32)


def _split3(x):
    hi = x.astype(BF16)
    r = x - hi.astype(F32)
    mid = r.astype(BF16)
    lo = (r - mid.astype(F32)).astype(BF16)
    return hi, mid, lo


def _rms_mod(x, g, shift, scale):
    ms = jnp.mean(x * x, axis=-1, keepdims=True)
    return (x * lax.rsqrt(ms + EPS) * g) * (1.0 + scale) + shift


def _silu(x):
    return x * (1.0 / (1.0 + jnp.exp(-x)))


def _softplus(x):
    return jnp.maximum(x, 0.0) + jnp.log1p(jnp.exp(-jnp.abs(x)))


def _mod_row(tile_rows):
    lat_tiles = T_LAT // tile_rows
    per_batch = SEQ // tile_rows

    def f(i):
        return jnp.where(i < lat_tiles, i // per_batch, CTX_MOD_ROW)

    return f


def _const_spec(shape):
    nd = len(shape)
    return pl.BlockSpec(shape, lambda *_: (0,) * nd, pipeline_mode=pl.Buffered(1))


def _ada_kernel(c_ref, w_ref, b_ref, o_ref):
    s = _silu(c_ref[...])
    s_hi = s.astype(BF16)
    s_lo = (s - s_hi.astype(F32)).astype(BF16)
    w = w_ref[0]
    w_hi = w.astype(BF16)
    w_lo = (w - w_hi.astype(F32)).astype(BF16)
    o_ref[0] = _dot(s_hi, w_hi) + _dot(s_hi, w_lo) + _dot(s_lo, w_hi) + b_ref[0]


def _ada(c_all, ada_w, ada_b):
    tn = 1024
    depth = ada_w.shape[0]
    n = ada_w.shape[2]
    return pl.pallas_call(
        _ada_kernel,
        grid=(depth, n // tn),
        in_specs=[
            pl.BlockSpec((MOD_ROWS, D), lambda l, j: (0, 0)),
            pl.BlockSpec((1, D, tn), lambda l, j: (l, 0, j)),
            pl.BlockSpec((1, 1, tn), lambda l, j: (l, 0, j)),
        ],
        out_specs=pl.BlockSpec((1, MOD_ROWS, tn), lambda l, j: (l, 0, j)),
        out_shape=jax.ShapeDtypeStruct((depth, MOD_ROWS, n), F32),
        compiler_params=_cparams(2),
        name="ada_mod",
    )(c_all, ada_w, ada_b.reshape(depth, 1, n))


def _ffn_kernel(x_ref, mod_ref, g_ref, w1_ref, w3_ref, w2_ref, *rest, j0, final_norm):
    if final_norm:
        fg_ref, o_ref = rest
    else:
        (o_ref,) = rest
    x = x_ref[...]
    shift = mod_ref[0, j0:j0 + 1, :]
    scale = mod_ref[0, j0 + 1:j0 + 2, :]
    gate = mod_ref[0, j0 + 2:j0 + 3, :]
    h = _rms_mod(x, g_ref[...], shift, scale).astype(BF16)
    h1 = _dot(h, w1_ref[...])
    h3 = _dot(h, w3_ref[...])
    a = (_silu(h1) * h3).astype(BF16)
    y = _dot(a, w2_ref[...])
    out = x + (0.5 * gate) * y
    if final_norm:
        ms = jnp.mean(out * out, axis=-1, keepdims=True)
        out = out * lax.rsqrt(ms + EPS) * fg_ref[...]
    o_ref[...] = out


def _ffn(x, mod, j0, g, w1, w3, w2, *, n_rows, final_g=None, tm=512):
    final_norm = final_g is not None
    in_specs = [
        pl.BlockSpec((tm, D), lambda i: (i, 0)),
        pl.BlockSpec((1, N_MOD, D), lambda i, f=_mod_row(tm): (f(i), 0, 0)),
        _const_spec((1, D)),
        _const_spec((D, D_FF)),
        _const_spec((D, D_FF)),
        _const_spec((D_FF, D)),
    ]
    args = [x, mod, g.reshape(1, D), w1, w3, w2]
    if final_norm:
        in_specs.append(_const_spec((1, D)))
        args.append(final_g.reshape(1, D))
    return pl.pallas_call(
        functools.partial(_ffn_kernel, j0=j0, final_norm=final_norm),
        grid=(n_rows // tm,),
        in_specs=in_specs,
        out_specs=pl.BlockSpec((tm, D), lambda i: (i, 0)),
        out_shape=jax.ShapeDtypeStruct((n_rows, D), F32),
        compiler_params=_cparams(1),
        name="ffn_final" if final_norm else "ffn",
    )(*args)


def _seq_tile_info(i):
    lat_tiles = T_LAT // SEQ_TILE
    per_seq = SEQ // SEQ_TILE
    is_lat = i < lat_tiles
    pos = jnp.where(is_lat, i % per_seq, 0)
    first = pos == 0
    last = jnp.where(is_lat, pos == per_seq - 1, True)
    seq_len = jnp.where(is_lat, SEQ, CTX)
    return first, last, pos * SEQ_TILE, seq_len


def _halo_specs(width, halo):
    blocks_per_tile = SEQ_TILE // halo
    n_blocks = N_TOK // halo
    prev = pl.BlockSpec((halo, width), lambda i: (jnp.maximum(i * blocks_per_tile - 1, 0), 0))
    nxt = pl.BlockSpec(
        (halo, width), lambda i: (jnp.minimum((i + 1) * blocks_per_tile, n_blocks - 1), 0))
    return prev, nxt


def _inproj_kernel(xp_ref, x_ref, xn_ref, mod_ref, g_ref, wz_ref, wx_ref, wu_ref, wd0_ref, wd1_ref,
                   wdt_ref, dbr_ref, dbc_ref, cw_ref, cb_ref,
                   z_ref, xs_ref, bm_ref, cm_ref, up_ref, dtc_ref, dtt_ref):
    first, last, _, _ = _seq_tile_info(pl.program_id(0))
    g = g_ref[...]
    shift = mod_ref[0, 3:4, :]
    scale = mod_ref[0, 4:5, :]
    hm = _rms_mod(x_ref[...], g, shift, scale)
    hp = jnp.where(first, 0.0, _rms_mod(xp_ref[...], g, shift, scale))
    hn = jnp.where(last, 0.0, _rms_mod(xn_ref[...], g, shift, scale))
    h = hm.astype(BF16)
    h_ext = jnp.concatenate([hp.astype(BF16), h, hn.astype(BF16)], axis=0)
    off = HALO - SSD_CONV // 2
    cw = CONV_COLS
    n_conv = CONV_DIM // cw
    plain = [(z_ref, wz_ref, c) for c in range(D // cw)] + [(up_ref, wu_ref, c) for c in range(D // cw)]
    for c in range(n_conv):
        cs = slice(c * cw, (c + 1) * cw)
        xbc = _dot(h_ext, wx_ref[:, cs])
        acc = cb_ref[:, cs] + cw_ref[0:1, cs] * xbc[off:off + SEQ_TILE]
        for k in range(1, SSD_CONV):
            acc = acc + cw_ref[k:k + 1, cs] * xbc[off + k:off + k + SEQ_TILE]
        act = _silu(acc).astype(BF16)
        if c * cw < D:
            xs_ref[:, cs] = act
        elif c * cw < D + 2 * SSD_STATE:
            bm_ref[...] = act
        else:
            cm_ref[...] = act
        for o_ref, w_ref, pc in plain[c * len(plain) // n_conv:(c + 1) * len(plain) // n_conv]:
            ps = slice(pc * cw, (pc + 1) * cw)
            o_ref[:, ps] = _dot(h, w_ref[:, ps]).astype(BF16)
    dtc_ref[0] = _softplus(_dot(h, wd0_ref[...]) + dbr_ref[0])
    dtc_ref[1] = _softplus(_dot(h, wd1_ref[...]) + dbr_ref[1])
    dtt_ref[...] = _softplus(_dot_nt(wdt_ref[...], h) + dbc_ref[...])


def _inproj(x, mod, g, wz, wx, wu, wd0, wd1, wdt, db_row, db_col, conv_w, conv_b):
    n = x.shape[0]
    tm = SEQ_TILE
    row = lambda i: (i, 0)
    prev, nxt = _halo_specs(D, HALO)
    act = lambda w: jax.ShapeDtypeStruct((n, w), BF16)
    return pl.pallas_call(
        _inproj_kernel,
        grid=(n // tm,),
        in_specs=[
            prev, pl.BlockSpec((tm, D), row), nxt,
            pl.BlockSpec((1, N_MOD, D), lambda i, f=_mod_row(tm): (f(i), 0, 0)),
            _const_spec((1, D)),
            _const_spec((D, D)),
            _const_spec((D, CONV_DIM)),
            _const_spec((D, D)),
            _const_spec((D, HEADS)),
            _const_spec((D, HEADS)),
            _const_spec((2 * HEADS, D)),
            _const_spec((2, 1, HEADS)),
            _const_spec((2 * HEADS, 1)),
            _const_spec((SSD_CONV, CONV_DIM)),
            _const_spec((1, CONV_DIM)),
        ],
        out_specs=[
            pl.BlockSpec((tm, D), row),
            pl.BlockSpec((tm, D), row),
            pl.BlockSpec((tm, 2 * SSD_STATE), row),
            pl.BlockSpec((tm, 2 * SSD_STATE), row),
            pl.BlockSpec((tm, D), row),
            pl.BlockSpec((2, tm, HEADS), lambda i: (0, i, 0)),
            pl.BlockSpec((2 * HEADS, tm), lambda i: (0, i)),
        ],
        out_shape=[
            act(D), act(D), act(2 * SSD_STATE), act(2 * SSD_STATE), act(D),
            jax.ShapeDtypeStruct((2, n, HEADS), F32),
            jax.ShapeDtypeStruct((2 * HEADS, n), F32),
        ],
        compiler_params=_cparams(1),
        name="ssd_inproj",
    )(x, x, x, mod, g.reshape(1, D), wz, wx, wu, wd0, wd1, wdt, db_row, db_col,
      conv_w, conv_b.reshape(1, CONV_DIM))


def _ssd_chunk(fwd, xs_ref, bm_ref, cm_ref, dtc_ref, dtt_ref, a_log_col, a_log_row, y_ref, state_ref):
    xs_bf = xs_ref[...]
    bm_bf = bm_ref[...]
    cm_bf = cm_ref[...]
    dtc = dtc_ref[0]
    dtt = dtt_ref[...]
    a_t = dtt * (-jnp.exp(a_log_col) * LOG2E)
    a_c = dtc * (-jnp.exp(a_log_row) * LOG2E)

    ri = lax.broadcasted_iota(jnp.int32, (CHUNK, CHUNK), 0)
    ci = lax.broadcasted_iota(jnp.int32, (CHUNK, CHUNK), 1)
    causal = (ci <= ri) if fwd else (ci >= ri)
    tri_ls = causal.astype(BF16)
    tri_ts = ((ri <= ci) if fwd else (ri >= ci)).astype(BF16)

    t_hi, t_mid, t_lo = _split3(a_t)
    cum_t = _dot(t_hi, tri_ts) + _dot(t_mid, tri_ts) + _dot(t_lo, tri_ts)
    c_hi, c_mid, c_lo = _split3(a_c)
    cum_c = _dot(tri_ls, c_hi) + _dot(tri_ls, c_mid) + _dot(tri_ls, c_lo)
    tot_c = jnp.sum(a_t, axis=1, keepdims=True)
    tot_r = jnp.sum(a_c, axis=0, keepdims=True)

    w_t = (jnp.exp2(tot_c - cum_t) * dtt).astype(BF16)
    src_t = cum_t - jnp.log2(dtt)

    eh = lax.broadcasted_iota(jnp.int32, (HEADS, D), 0)
    ec = lax.broadcasted_iota(jnp.int32, (HEADS, D), 1)
    expand = ((ec >= eh * HEAD_DIM) & (ec < (eh + 1) * HEAD_DIM)).astype(BF16)
    cd = jnp.broadcast_to(jnp.exp2(tot_r), (8, HEADS))
    cd_hi = cd.astype(BF16)
    cd_lo = (cd - cd_hi.astype(F32)).astype(BF16)
    chunk_decay = (_dot(cd_hi, expand) + _dot(cd_lo, expand))[0:1]

    state = state_ref[...]
    state_bf = state.astype(BF16)
    lane = lax.broadcasted_iota(jnp.int32, (CHUNK, 2 * HEAD_DIM), 1)
    low_half = lane < HEAD_DIM
    neg_inf = jnp.float32(-jnp.inf)

    def block_diag(pair):
        zero = jnp.zeros_like(pair)
        return jnp.concatenate(
            [jnp.where(low_half, pair, zero), jnp.where(low_half, zero, pair)], axis=0)

    for g in range(2):
        gs = slice(g * SSD_STATE, (g + 1) * SSD_STATE)
        bg_bf = bm_bf[:, gs]
        cg_bf = cm_bf[:, gs]
        cb_bf = _dot_nt(cg_bf, bg_bf).astype(BF16)
        bgt_bf = bg_bf.astype(F32).T.astype(BF16)
        for pr in range(HEADS // 4):
            h0 = g * (HEADS // 2) + 2 * pr
            sl = slice(h0 * HEAD_DIM, (h0 + 2) * HEAD_DIM)
            rhs_x = block_diag(xs_bf[:, sl])
            rhs_s = block_diag(state_bf[:, sl])

            def head_lhs(h):
                col = jnp.broadcast_to(cum_c[:, h:h + 1], (CHUNK, CHUNK))
                seg = jnp.exp2(jnp.where(causal, col - src_t[h:h + 1, :], neg_inf))
                intra = cb_bf * seg.astype(BF16)
                read = cg_bf * jnp.exp2(col).astype(BF16)
                return intra, read

            def to_state(h):
                return bgt_bf * w_t[h:h + 1, :]

            i0, r0 = head_lhs(h0)
            i1, r1 = head_lhs(h0 + 1)
            y = _dot(jnp.concatenate([i0, i1, r0, r1], axis=1),
                     jnp.concatenate([rhs_x, rhs_s], axis=0))
            s_new = _dot(jnp.concatenate([to_state(h0), to_state(h0 + 1)], axis=1), rhs_x)
            y_ref[:, sl] = y.astype(BF16)
            state_ref[:, sl] = state[:, sl] * chunk_decay[:, sl] + s_new


def _ssd_kernel(xs_f, bm_f, cm_f, dtc_f, dtt_f, xs_b, bm_b, cm_b, dtc_b, dtt_b, alc_ref, alr_ref,
                yf_ref, yb_ref, state_ref):
    @pl.when(pl.program_id(1) == 0)
    def _():
        state_ref[...] = jnp.zeros_like(state_ref)

    _ssd_chunk(True, xs_f, bm_f, cm_f, dtc_f, dtt_f, alc_ref[0], alr_ref[0], yf_ref, state_ref.at[0])
    _ssd_chunk(False, xs_b, bm_b, cm_b, dtc_b, dtt_b, alc_ref[1], alr_ref[1], yb_ref, state_ref.at[1])


def _ssd_chunk_index(b, d, k):
    lat_chunks = SEQ // CHUNK
    ctx_chunks = CTX // CHUNK
    pos = jnp.where(d == 0, k,
                    jnp.where(k < ctx_chunks, ctx_chunks - 1 - k, 2 * ctx_chunks + lat_chunks - 1 - k))
    return jnp.where(pos < ctx_chunks,
                     T_LAT // CHUNK + ctx_chunks * b + pos,
                     lat_chunks * b + pos - ctx_chunks)


def _ssd(xs, bm, cm, dtc, dtt, a_log):
    n = xs.shape[0]
    steps = (SEQ + CTX) // CHUNK
    cidx = _ssd_chunk_index

    def dir_specs(d):
        blk = lambda b, k: (cidx(b, d, k), 0)
        return [
            pl.BlockSpec((CHUNK, D), blk),
            pl.BlockSpec((CHUNK, 2 * SSD_STATE), blk),
            pl.BlockSpec((CHUNK, 2 * SSD_STATE), blk),
            pl.BlockSpec((1, CHUNK, HEADS), lambda b, k: (d, cidx(b, d, k), 0)),
            pl.BlockSpec((HEADS, CHUNK), lambda b, k: (d, cidx(b, d, k))),
        ]

    y = jax.ShapeDtypeStruct((n, D), BF16)
    return pl.pallas_call(
        _ssd_kernel,
        grid=(BATCH, steps),
        in_specs=dir_specs(0) + dir_specs(1) + [_const_spec((2, HEADS, 1)), _const_spec((2, 1, HEADS))],
        out_specs=[pl.BlockSpec((CHUNK, D), lambda b, k: (cidx(b, 0, k), 0)),
                   pl.BlockSpec((CHUNK, D), lambda b, k: (cidx(b, 1, k), 0))],
        out_shape=[y, y],
        scratch_shapes=[pltpu.VMEM((2, SSD_STATE, D), F32)],
        compiler_params=_cparams(2),
        name="ssd_scan",
    )(xs, bm, cm, dtc, dtt, xs, bm, cm, dtc, dtt,
      a_log.reshape(2, HEADS, 1), a_log.reshape(2, 1, HEADS))


def _window_sum(e, w):
    acc = e
    span = 1
    while span < w:
        n = acc.shape[0] - span
        acc = acc[:n] + acc[span:span + n]
        span *= 2
    start = HALO - w // 2
    return acc[start:start + SEQ_TILE]


def _finish_kernel(x_ref, mod_ref, yf_ref, yb_ref, xs_ref, z_ref, uprev_ref, up_ref, unext_ref,
                   dsk_ref, gn_ref, pw_ref, ps_ref, ow_ref, o_ref):
    first, last, row0, seq_len = _seq_tile_info(pl.program_id(0))
    y = yf_ref[...].astype(F32) + yb_ref[...].astype(F32) + dsk_ref[...] * xs_ref[...].astype(F32)
    y = y * _silu(z_ref[...].astype(F32))
    ms = jnp.mean(y * y, axis=-1, keepdims=True)
    yn = (y * lax.rsqrt(ms + EPS) * gn_ref[...]).astype(BF16)
    out = _dot(yn, ow_ref[0:D, :])
    up = up_ref[...].astype(F32)
    prev = jnp.where(first, 0.0, uprev_ref[...].astype(F32))
    nxt = jnp.where(last, 0.0, unext_ref[...].astype(F32))
    ext = jnp.concatenate([prev, up, nxt], axis=0)
    pos = row0 + lax.broadcasted_iota(jnp.int32, (SEQ_TILE, POOL_GROUP), 0)
    for g, w in enumerate(POOL_WINDOWS):
        sl = slice(g * POOL_GROUP, (g + 1) * POOL_GROUP)
        lo = jnp.maximum(pos - w // 2, 0)
        hi = jnp.minimum(pos + (w - 1 - w // 2), seq_len - 1)
        cnt = (hi - lo + 1).astype(F32)
        mean = _window_sum(ext[:, sl], w) / cnt
        m = (mean - up[:, sl]).astype(BF16)
        pooled = (_dot(m, pw_ref[g]) * ps_ref[:, sl]).astype(BF16)
        out = out + _dot(pooled, ow_ref[D + g * POOL_GROUP:D + (g + 1) * POOL_GROUP, :])
    o_ref[...] = x_ref[...] + mod_ref[0, 5:6, :] * out


def _finish(x, mod, yf, yb, xs, z, up, d_skip, gn_g, pool_w, pool_scale, out_w):
    n = x.shape[0]
    prev, nxt = _halo_specs(D, HALO)
    row = lambda i: (i, 0)
    tile = pl.BlockSpec((SEQ_TILE, D), row)
    return pl.pallas_call(
        _finish_kernel,
        grid=(n // SEQ_TILE,),
        in_specs=[
            tile,
            pl.BlockSpec((1, N_MOD, D), lambda i, f=_mod_row(SEQ_TILE): (f(i), 0, 0)),
            tile, tile, tile, tile, prev, tile, nxt,
            _const_spec((1, D)), _const_spec((1, D)),
            _const_spec((N_POOL, POOL_GROUP, POOL_GROUP)), _const_spec((1, D)),
            _const_spec((2 * D, D)),
        ],
        out_specs=tile,
        out_shape=jax.ShapeDtypeStruct((n, D), F32),
        compiler_params=_cparams(1),
        name="ssd_finish",
    )(x, mod, yf, yb, xs, z, up, up, up, d_skip, gn_g.reshape(1, D), pool_w,
      pool_scale.reshape(1, D), out_w)


def _qkv_kernel(x_ref, mod_ref, g_ref, w_ref, q_ref, k_ref, v_ref):
    h = _rms_mod(x_ref[...], g_ref[...], mod_ref[0, 3:4, :], mod_ref[0, 4:5, :]).astype(BF16)
    t = _dot(h, w_ref[...])
    q_ref[...] = (t[:, :D] * (HEAD_DIM ** -0.5)).astype(BF16)
    k_ref[...] = t[:, D:2 * D].astype(BF16)
    v_ref[...] = t[:, 2 * D:].astype(BF16)


def _qkv(x, mod, g, w, tm=512):
    n = x.shape[0]
    row = lambda i: (i, 0)
    out = jax.ShapeDtypeStruct((n, D), BF16)
    return pl.pallas_call(
        _qkv_kernel,
        grid=(n // tm,),
        in_specs=[pl.BlockSpec((tm, D), row),
                  pl.BlockSpec((1, N_MOD, D), lambda i, f=_mod_row(tm): (f(i), 0, 0)),
                  _const_spec((1, D)), _const_spec((D, 3 * D))],
        out_specs=[pl.BlockSpec((tm, D), row)] * 3,
        out_shape=[out, out, out],
        compiler_params=_cparams(1),
        name="na_qkv",
    )(x, mod, g.reshape(1, D), w)


NA_QROWS = 4
NA_GROUPS = GRID_ROWS // NA_QROWS
NA_KTILES = 3
NA_QTOK = NA_QROWS * GRID_W
NA_KTOK = NA_KTILES * NA_QROWS * GRID_W
NA_KINDS = 3


def _na_first_tile(m):
    return jnp.clip(m - 1, 0, NA_GROUPS - NA_KTILES)


def _na_kernel(q_ref, k_ref, v_ref, kc_ref, vc_ref, bias_ref, o_ref):
    lane = lax.broadcasted_iota(jnp.int32, (NA_QTOK, 2 * HEAD_DIM), 1)
    low_half = lane < HEAD_DIM
    kc = kc_ref[...]
    vc = vc_ref[...]

    def group(m, carry):
        kind = jnp.where(m == 0, 0, jnp.where(m == NA_GROUPS - 1, 2, 1))
        q0 = pl.multiple_of(m * NA_QTOK, NA_QTOK)
        k0 = pl.multiple_of(_na_first_tile(m) * NA_QTOK, NA_QTOK)
        q4 = q_ref[pl.ds(q0, NA_QTOK), :]
        zero = jnp.zeros_like(q4)
        qs = jnp.concatenate([jnp.where(low_half, q4, zero), jnp.where(low_half, zero, q4)], axis=0)
        k_win = k_ref[pl.ds(k0, NA_KTOK), :]
        v_win = v_ref[pl.ds(k0, NA_KTOK), :]
        s_win = _dot_nt(qs, k_win) + bias_ref[0, kind]
        s_ctx = _dot_nt(qs, kc)
        mx = jnp.maximum(jnp.max(s_win, axis=1, keepdims=True),
                         jnp.max(s_ctx, axis=1, keepdims=True))
        p_win = jnp.exp(s_win - mx)
        p_ctx = jnp.exp(s_ctx - mx)
        den = jnp.sum(p_win, axis=1, keepdims=True) + jnp.sum(p_ctx, axis=1, keepdims=True)
        o = (_dot(p_win.astype(BF16), v_win) + _dot(p_ctx.astype(BF16), vc)) / den
        o_ref[pl.ds(q0, NA_QTOK), :] = jnp.where(low_half, o[:NA_QTOK], o[NA_QTOK:]).astype(BF16)
        return carry

    lax.fori_loop(0, NA_GROUPS, group, 0, unroll=8)


def _na(q, k, v, bias):
    pair = 2 * HEAD_DIM
    ctx_blk0 = T_LAT // CTX
    seq = pl.BlockSpec((SEQ, pair), lambda hp, b: (b, hp))
    ctx = pl.BlockSpec((CTX, pair), lambda hp, b: (ctx_blk0 + b, hp))
    return pl.pallas_call(
        _na_kernel,
        grid=(HEADS // 2, BATCH),
        in_specs=[seq, seq, seq, ctx, ctx,
                  pl.BlockSpec((1, NA_KINDS, 2 * NA_QTOK, NA_KTOK), lambda hp, b: (hp, 0, 0, 0))],
        out_specs=seq,
        out_shape=jax.ShapeDtypeStruct((T_LAT, D), BF16),
        compiler_params=_cparams(2),
        name="na_attn",
    )(q, k, v, k, v, bias)


def _na_bias_table(rpb):
    j = np.arange(GRID_W)[:, None]
    kcol = np.arange(GRID_W)[None, :]
    cs = np.clip(j - WIN_COLS // 2, 0, GRID_W - WIN_COLS)
    col_ok = (kcol >= cs) & (kcol < cs + WIN_COLS)
    n_rel = 2 * WIN_COLS - 1
    left = GRID_W - WIN_COLS
    width = 2 * GRID_W
    r = jnp.pad(rpb.astype(F32), ((0, 0), (0, 0), (left, width - n_rel - left)))
    flat = jnp.tile(r, (1, 1, GRID_W))[:, :, :GRID_W * (width - 1)]
    toep = flat.reshape(HEADS, 2 * WIN_ROWS - 1, GRID_W, width - 1)[..., GRID_W - 1:]
    toep = jnp.where(col_ok[None, None], toep, -jnp.inf)
    masked = jnp.full((HEADS, GRID_W, GRID_W), -jnp.inf, F32)
    kinds = []
    for m in (0, NA_GROUPS // 2, NA_GROUPS - 1):
        t0 = int(np.clip(m - 1, 0, NA_GROUPS - NA_KTILES))
        per_q = []
        for qi in range(NA_QROWS):
            i = m * NA_QROWS + qi
            rs = int(np.clip(i - WIN_ROWS // 2, 0, GRID_ROWS - WIN_ROWS))
            blocks = []
            for a in range(NA_KTILES * NA_QROWS):
                kr = t0 * NA_QROWS + a
                blocks.append(toep[:, kr - i + WIN_ROWS - 1] if rs <= kr < rs + WIN_ROWS else masked)
            per_q.append(jnp.stack(blocks, axis=2).reshape(HEADS, GRID_W, NA_KTOK))
        kinds.append(jnp.stack(per_q, axis=1))
    t = jnp.stack(kinds, axis=1)
    t = t.reshape(HEADS // 2, 2, NA_KINDS, NA_QTOK, NA_KTOK)
    return jnp.transpose(t, (0, 2, 1, 3, 4)).reshape(HEADS // 2, NA_KINDS, 2 * NA_QTOK, NA_KTOK)


def _linres_kernel(x_ref, mod_ref, a_ref, w_ref, o_ref):
    o_ref[...] = x_ref[...] + mod_ref[0, 5:6, :] * _dot(a_ref[...], w_ref[...])


def _linres(x, mod, a, w, *, n_rows, tm=512):
    row = lambda i: (i, 0)
    return pl.pallas_call(
        _linres_kernel,
        grid=(n_rows // tm,),
        in_specs=[pl.BlockSpec((tm, D), row),
                  pl.BlockSpec((1, N_MOD, D), lambda i, f=_mod_row(tm): (f(i), 0, 0)),
                  pl.BlockSpec((tm, D), row), _const_spec((D, D))],
        out_specs=pl.BlockSpec((tm, D), row),
        out_shape=jax.ShapeDtypeStruct((n_rows, D), F32),
        compiler_params=_cparams(1),
        name="na_out",
    )(x, mod, a, w)


def kernel(x, c, ctx, c_ctx, ada_w, ada_b, norm_g, ffn_w1, ffn_w3, ffn_w2, ssd_in_w, ssd_conv_w,
           ssd_conv_b, ssd_dt_bias, ssd_a_log, ssd_d, ssd_norm_g, pool_w, pool_scale, mix_a_out_w,
           na_qkv_w, na_rpb, na_out_w, final_norm_g):
    xf = jnp.concatenate([x.reshape(T_LAT, D), ctx.reshape(T_CTX, D)], axis=0)
    c_all = jnp.concatenate(
        [c, c_ctx[None, :], jnp.zeros((MOD_ROWS - BATCH - 1, D), F32)], axis=0)
    mod = _ada(c_all, ada_w, ada_b).reshape(2, MOD_ROWS, N_MOD, D)
    w1 = ffn_w1.astype(BF16)
    w3 = ffn_w3.astype(BF16)
    w2 = ffn_w2.astype(BF16)

    m0 = mod[0]
    xf = _ffn(xf, m0, 0, norm_g[0, 0], w1[0, 0], w3[0, 0], w2[0, 0], n_rows=N_TOK)
    in_w = ssd_in_w[0]
    o1, o2, o3 = D, D + CONV_DIM, D + CONV_DIM + 2 * HEADS
    w_dt = in_w[:, o2:o3]
    z, xs, bm, cm, up, dtc, dtt = _inproj(
        xf, m0, norm_g[0, 1],
        in_w[:, :o1].astype(BF16), in_w[:, o1:o2].astype(BF16), in_w[:, o3:].astype(BF16),
        w_dt[:, :HEADS].astype(BF16), w_dt[:, HEADS:].astype(BF16), w_dt.T.astype(BF16),
        ssd_dt_bias[0].reshape(2, 1, HEADS), ssd_dt_bias[0].reshape(2 * HEADS, 1),
        ssd_conv_w[0], ssd_conv_b[0])
    yf, yb = _ssd(xs, bm, cm, dtc, dtt, ssd_a_log[0])
    d_skip = jnp.repeat(ssd_d[0].astype(F32), HEAD_DIM).reshape(1, D)
    xf = _finish(xf, m0, yf, yb, xs, z, up, d_skip, ssd_norm_g[0], pool_w[0].astype(BF16),
                 pool_scale[0], mix_a_out_w[0].astype(BF16))
    xf = _ffn(xf, m0, 6, norm_g[0, 2], w1[0, 1], w3[0, 1], w2[0, 1], n_rows=N_TOK)

    m1 = mod[1]
    xf = _ffn(xf, m1, 0, norm_g[1, 0], w1[1, 0], w3[1, 0], w2[1, 0], n_rows=N_TOK)
    q, k, v = _qkv(xf, m1, norm_g[1, 1], na_qkv_w[0].astype(BF16))
    att = _na(q, k, v, _na_bias_table(na_rpb[0]))
    xl = _linres(xf, m1, att, na_out_w[0].astype(BF16), n_rows=T_LAT)
    xl = _ffn(xl, m1, 6, norm_g[1, 2], w1[1, 1], w3[1, 1], w2[1, 1], n_rows=T_LAT,
              final_g=final_norm_g)
    return xl.reshape(BATCH, SEQ, D)
```

```python
import functools

import numpy as np
import jax
import jax.numpy as jnp
from jax import lax
from jax.experimental import pallas as pl
from jax.experimental.pallas import tpu as pltpu

F32 = jnp.float32
BF16 = jnp.bfloat16

D = 1024
BATCH = 8
SEQ = 4096
CTX = 256
N_MOD = 9
EPS = 1e-6
D_FF = 2816
FFN_TILE = 512
T_LAT = BATCH * SEQ
T_CTX = BATCH * CTX
N_TOK = T_LAT + T_CTX
MOD_ROWS = 16
CTX_MOD_ROW = BATCH

HEADS = 16
HEAD_DIM = 64
SSD_STATE = 128
SSD_CONV = 5
CHUNK = 128
CONV_DIM = 1536
CONV_COLS = 256
N_POOL = 4
POOL_GROUP = 256
POOL_WINDOWS = (2, 4, 8, 16)

GRID_W = 64
GRID_ROWS = SEQ // GRID_W
WIN_ROWS = 8
WIN_COLS = 16

HALO = 16
SEQ_TILE = 256
VMEM_LIMIT = 56 * 1024 * 1024
LOG2E = float(np.log2(np.e))


def _cparams(n_axes):
    return pltpu.CompilerParams(
        dimension_semantics=("arbitrary",) * n_axes, vmem_limit_bytes=VMEM_LIMIT)


def _dot(a, b):
    return jnp.dot(a, b, preferred_element_type=F32)


def _dot_nt(a, b):
    return lax.dot_general(a, b, (((1,), (1,)), ((), ())), preferred_element_type=F32)


def _split3(x):
    hi = x.astype(BF16)
    r = x - hi.astype(F32)
    mid = r.astype(BF16)
    lo = (r - mid.astype(F32)).astype(BF16)
    return hi, mid, lo


def _rms_mod(x, g, shift, scale):
    ms = jnp.mean(x * x, axis=-1, keepdims=True)
    return (x * lax.rsqrt(ms + EPS) * g) * (1.0 + scale) + shift


def _silu(x):
    return x * (1.0 / (1.0 + jnp.exp(-x)))


def _softplus(x):
    return jnp.maximum(x, 0.0) + jnp.log1p(jnp.exp(-jnp.abs(x)))


def _mod_row(tile_rows):
    lat_tiles = T_LAT // tile_rows
    per_batch = SEQ // tile_rows

    def f(i):
        return jnp.where(i < lat_tiles, i // per_batch, CTX_MOD_ROW)

    return f


def _const_spec(shape):
    nd = len(shape)
    return pl.BlockSpec(shape, lambda *_: (0,) * nd, pipeline_mode=pl.Buffered(1))


def _ada_kernel(c_ref, w_ref, b_ref, o_ref):
    s = _silu(c_ref[...])
    s_hi = s.astype(BF16)
    s_lo = (s - s_hi.astype(F32)).astype(BF16)
    w = w_ref[0]
    w_hi = w.astype(BF16)
    w_lo = (w - w_hi.astype(F32)).astype(BF16)
    o_ref[0] = _dot(s_hi, w_hi) + _dot(s_hi, w_lo) + _dot(s_lo, w_hi) + b_ref[0]


def _ada(c_all, ada_w, ada_b):
    tn = 1024
    depth = ada_w.shape[0]
    n = ada_w.shape[2]
    return pl.pallas_call(
        _ada_kernel,
        grid=(depth, n // tn),
        in_specs=[
            pl.BlockSpec((MOD_ROWS, D), lambda l, j: (0, 0)),
            pl.BlockSpec((1, D, tn), lambda l, j: (l, 0, j)),
            pl.BlockSpec((1, 1, tn), lambda l, j: (l, 0, j)),
        ],
        out_specs=pl.BlockSpec((1, MOD_ROWS, tn), lambda l, j: (l, 0, j)),
        out_shape=jax.ShapeDtypeStruct((depth, MOD_ROWS, n), F32),
        compiler_params=_cparams(2),
        name="ada_mod",
    )(c_all, ada_w, ada_b.reshape(depth, 1, n))


def _ffn_kernel(*refs, j0, final_norm, split_rows, mixer_out):
    refs = list(refs)
    x_ref = refs.pop(0)
    xc_ref = refs.pop(0) if split_rows else None
    att_ref, wo_ref = (refs.pop(0), refs.pop(0)) if mixer_out else (None, None)
    mod_ref, g_ref, w1_ref, w3_ref, w2_ref = refs[:5]
    fg_ref = refs[5] if final_norm else None
    o_ref = refs[-1]
    if split_rows:
        x = jnp.where(pl.program_id(0) * x_ref.shape[0] >= split_rows, xc_ref[...], x_ref[...])
    else:
        x = x_ref[...]
    if mixer_out:
        x = x + mod_ref[0, 5:6, :] * _dot(att_ref[...], wo_ref[...])
    shift = mod_ref[0, j0:j0 + 1, :]
    scale = mod_ref[0, j0 + 1:j0 + 2, :]
    gate = mod_ref[0, j0 + 2:j0 + 3, :]
    h = _rms_mod(x, g_ref[...], shift, scale).astype(BF16)
    h1 = _dot(h, w1_ref[...])
    h3 = _dot(h, w3_ref[...])
    a = (_silu(h1) * h3).astype(BF16)
    y = _dot(a, w2_ref[...])
    out = x + (0.5 * gate) * y
    if final_norm:
        ms = jnp.mean(out * out, axis=-1, keepdims=True)
        out = out * lax.rsqrt(ms + EPS) * fg_ref[...]
    o_ref[...] = out


def _ffn(x, mod, j0, g, w1, w3, w2, *, n_rows, final_g=None, x_ctx=None, mixer_out=None, tm=FFN_TILE):
    final_norm = final_g is not None
    split_rows = x.shape[0] if x_ctx is not None else 0
    if x_ctx is None:
        x_specs, xs = [pl.BlockSpec((tm, D), lambda i: (i, 0))], [x]
    else:
        lat_tiles = split_rows // tm
        x_specs = [pl.BlockSpec((tm, D), lambda i: (jnp.minimum(i, lat_tiles - 1), 0)),
                   pl.BlockSpec((tm, D), lambda i: (jnp.maximum(i - lat_tiles, 0), 0))]
        xs = [x, x_ctx]
    if mixer_out is not None:
        x_specs += [pl.BlockSpec((tm, D), lambda i: (i, 0)), _const_spec((D, D))]
        xs += list(mixer_out)
    in_specs = x_specs + [
        pl.BlockSpec((1, N_MOD, D), lambda i, f=_mod_row(tm): (f(i), 0, 0)),
        _const_spec((1, D)),
        _const_spec((D, D_FF)),
        _const_spec((D, D_FF)),
        _const_spec((D_FF, D)),
    ]
    args = xs + [mod, g.reshape(1, D), w1, w3, w2]
    if final_norm:
        in_specs.append(_const_spec((1, D)))
        args.append(final_g.reshape(1, D))
    return pl.pallas_call(
        functools.partial(_ffn_kernel, j0=j0, final_norm=final_norm, split_rows=split_rows,
                          mixer_out=mixer_out is not None),
        grid=(n_rows // tm,),
        in_specs=in_specs,
        out_specs=pl.BlockSpec((tm, D), lambda i: (i, 0)),
        out_shape=jax.ShapeDtypeStruct((n_rows, D), F32),
        compiler_params=_cparams(1),
        name="ffn_final" if final_norm else "ffn",
    )(*args)


def _seq_tile_info(i):
    lat_tiles = T_LAT // SEQ_TILE
    per_seq = SEQ // SEQ_TILE
    is_lat = i < lat_tiles
    pos = jnp.where(is_lat, i % per_seq, 0)
    first = pos == 0
    last = jnp.where(is_lat, pos == per_seq - 1, True)
    seq_len = jnp.where(is_lat, SEQ, CTX)
    return first, last, pos * SEQ_TILE, seq_len


def _halo_specs(width, halo):
    blocks_per_tile = SEQ_TILE // halo
    n_blocks = N_TOK // halo
    prev = pl.BlockSpec((halo, width), lambda i: (jnp.maximum(i * blocks_per_tile - 1, 0), 0))
    nxt = pl.BlockSpec(
        (halo, width), lambda i: (jnp.minimum((i + 1) * blocks_per_tile, n_blocks - 1), 0))
    return prev, nxt


def _inproj_kernel(xp_ref, x_ref, xn_ref, mod_ref, g_ref, wz_ref, wx_ref, wu_ref, wd0_ref, wd1_ref,
                   wdt_ref, dbr_ref, dbc_ref, cw_ref, cb_ref,
                   z_ref, xs_ref, bm_ref, cm_ref, up_ref, dtc_ref, dtt_ref):
    first, last, _, _ = _seq_tile_info(pl.program_id(0))
    g = g_ref[...]
    shift = mod_ref[0, 3:4, :]
    scale = mod_ref[0, 4:5, :]
    hm = _rms_mod(x_ref[...], g, shift, scale)
    hp = jnp.where(first, 0.0, _rms_mod(xp_ref[...], g, shift, scale))
    hn = jnp.where(last, 0.0, _rms_mod(xn_ref[...], g, shift, scale))
    h = hm.astype(BF16)
    h_ext = jnp.concatenate([hp.astype(BF16), h, hn.astype(BF16)], axis=0)
    off = HALO - SSD_CONV // 2
    cw = CONV_COLS
    n_conv = CONV_DIM // cw
    plain = [(z_ref, wz_ref, c) for c in range(D // cw)] + [(up_ref, wu_ref, c) for c in range(D // cw)]
    for c in range(n_conv):
        cs = slice(c * cw, (c + 1) * cw)
        xbc = _dot(h_ext, wx_ref[:, cs])
        acc = cb_ref[:, cs] + cw_ref[0:1, cs] * xbc[off:off + SEQ_TILE]
        for k in range(1, SSD_CONV):
            acc = acc + cw_ref[k:k + 1, cs] * xbc[off + k:off + k + SEQ_TILE]
        act = _silu(acc).astype(BF16)
        if c * cw < D:
            xs_ref[:, cs] = act
        elif c * cw < D + 2 * SSD_STATE:
            bm_ref[...] = act
        else:
            cm_ref[...] = act
        for o_ref, w_ref, pc in plain[c * len(plain) // n_conv:(c + 1) * len(plain) // n_conv]:
            ps = slice(pc * cw, (pc + 1) * cw)
            o_ref[:, ps] = _dot(h, w_ref[:, ps]).astype(BF16)
    dtc_ref[0] = _softplus(_dot(h, wd0_ref[...]) + dbr_ref[0])
    dtc_ref[1] = _softplus(_dot(h, wd1_ref[...]) + dbr_ref[1])
    dtt_ref[...] = _softplus(_dot_nt(wdt_ref[...], h) + dbc_ref[...])


def _inproj(x, mod, g, wz, wx, wu, wd0, wd1, wdt, db_row, db_col, conv_w, conv_b):
    n = x.shape[0]
    tm = SEQ_TILE
    row = lambda i: (i, 0)
    prev, nxt = _halo_specs(D, HALO)
    act = lambda w: jax.ShapeDtypeStruct((n, w), BF16)
    return pl.pallas_call(
        _inproj_kernel,
        grid=(n // tm,),
        in_specs=[
            prev, pl.BlockSpec((tm, D), row), nxt,
            pl.BlockSpec((1, N_MOD, D), lambda i, f=_mod_row(tm): (f(i), 0, 0)),
            _const_spec((1, D)),
            _const_spec((D, D)),
            _const_spec((D, CONV_DIM)),
            _const_spec((D, D)),
            _const_spec((D, HEADS)),
            _const_spec((D, HEADS)),
            _const_spec((2 * HEADS, D)),
            _const_spec((2, 1, HEADS)),
            _const_spec((2 * HEADS, 1)),
            _const_spec((SSD_CONV, CONV_DIM)),
            _const_spec((1, CONV_DIM)),
        ],
        out_specs=[
            pl.BlockSpec((tm, D), row),
            pl.BlockSpec((tm, D), row),
            pl.BlockSpec((tm, 2 * SSD_STATE), row),
            pl.BlockSpec((tm, 2 * SSD_STATE), row),
            pl.BlockSpec((tm, D), row),
            pl.BlockSpec((2, tm, HEADS), lambda i: (0, i, 0)),
            pl.BlockSpec((2 * HEADS, tm), lambda i: (0, i)),
        ],
        out_shape=[
            act(D), act(D), act(2 * SSD_STATE), act(2 * SSD_STATE), act(D),
            jax.ShapeDtypeStruct((2, n, HEADS), F32),
            jax.ShapeDtypeStruct((2 * HEADS, n), F32),
        ],
        compiler_params=_cparams(1),
        name="ssd_inproj",
    )(x, x, x, mod, g.reshape(1, D), wz, wx, wu, wd0, wd1, wdt, db_row, db_col,
      conv_w, conv_b.reshape(1, CONV_DIM))


def _ssd_chunk(fwd, xs_ref, bm_ref, cm_ref, dtc_ref, dtt_ref, a_log_col, a_log_row, y_ref, state_ref):
    xs_bf = xs_ref[...]
    bm_bf = bm_ref[...]
    cm_bf = cm_ref[...]
    dtc = dtc_ref[0]
    dtt = dtt_ref[...]
    a_t = dtt * (-jnp.exp(a_log_col) * LOG2E)
    a_c = dtc * (-jnp.exp(a_log_row) * LOG2E)

    ri = lax.broadcasted_iota(jnp.int32, (CHUNK, CHUNK), 0)
    ci = lax.broadcasted_iota(jnp.int32, (CHUNK, CHUNK), 1)
    causal = (ci <= ri) if fwd else (ci >= ri)
    tri_ls = causal.astype(BF16)
    tri_ts = ((ri <= ci) if fwd else (ri >= ci)).astype(BF16)

    t_hi, t_mid, t_lo = _split3(a_t)
    cum_t = _dot(t_hi, tri_ts) + _dot(t_mid, tri_ts) + _dot(t_lo, tri_ts)
    c_hi, c_mid, c_lo = _split3(a_c)
    cum_c = _dot(tri_ls, c_hi) + _dot(tri_ls, c_mid) + _dot(tri_ls, c_lo)
    tot_c = jnp.sum(a_t, axis=1, keepdims=True)
    tot_r = jnp.sum(a_c, axis=0, keepdims=True)

    w_t = (jnp.exp2(tot_c - cum_t) * dtt).astype(BF16)
    src_t = cum_t - jnp.log2(dtt)

    eh = lax.broadcasted_iota(jnp.int32, (HEADS, D), 0)
    ec = lax.broadcasted_iota(jnp.int32, (HEADS, D), 1)
    expand = ((ec >= eh * HEAD_DIM) & (ec < (eh + 1) * HEAD_DIM)).astype(BF16)
    cd = jnp.broadcast_to(jnp.exp2(tot_r), (8, HEADS))
    cd_hi = cd.astype(BF16)
    cd_lo = (cd - cd_hi.astype(F32)).astype(BF16)
    chunk_decay = (_dot(cd_hi, expand) + _dot(cd_lo, expand))[0:1]

    state = state_ref[...]
    state_bf = state.astype(BF16)
    lane = lax.broadcasted_iota(jnp.int32, (CHUNK, 2 * HEAD_DIM), 1)
    low_half = lane < HEAD_DIM
    neg_inf = jnp.float32(-jnp.inf)

    def block_diag(pair):
        zero = jnp.zeros_like(pair)
        return jnp.concatenate(
            [jnp.where(low_half, pair, zero), jnp.where(low_half, zero, pair)], axis=0)

    for g in range(2):
        gs = slice(g * SSD_STATE, (g + 1) * SSD_STATE)
        bg_bf = bm_bf[:, gs]
        cg_bf = cm_bf[:, gs]
        cb_bf = _dot_nt(cg_bf, bg_bf).astype(BF16)
        bgt_bf = bg_bf.astype(F32).T.astype(BF16)
        for pr in range(HEADS // 4):
            h0 = g * (HEADS // 2) + 2 * pr
            sl = slice(h0 * HEAD_DIM, (h0 + 2) * HEAD_DIM)
            rhs_x = block_diag(xs_bf[:, sl])
            rhs_s = block_diag(state_bf[:, sl])

            def head_lhs(h):
                col = jnp.broadcast_to(cum_c[:, h:h + 1], (CHUNK, CHUNK))
                seg = jnp.exp2(jnp.where(causal, col - src_t[h:h + 1, :], neg_inf))
                intra = cb_bf * seg.astype(BF16)
                read = cg_bf * jnp.exp2(col).astype(BF16)
                return intra, read

            def to_state(h):
                return bgt_bf * w_t[h:h + 1, :]

            i0, r0 = head_lhs(h0)
            i1, r1 = head_lhs(h0 + 1)
            y = _dot(jnp.concatenate([i0, i1, r0, r1], axis=1),
                     jnp.concatenate([rhs_x, rhs_s], axis=0))
            s_new = _dot(jnp.concatenate([to_state(h0), to_state(h0 + 1)], axis=1), rhs_x)
            y_ref[:, sl] = y.astype(BF16)
            state_ref[:, sl] = state[:, sl] * chunk_decay[:, sl] + s_new


def _ssd_kernel(xs_f, bm_f, cm_f, dtc_f, dtt_f, xs_b, bm_b, cm_b, dtc_b, dtt_b, alc_ref, alr_ref,
                yf_ref, yb_ref, state_ref):
    @pl.when(pl.program_id(1) == 0)
    def _():
        state_ref[...] = jnp.zeros_like(state_ref)

    _ssd_chunk(True, xs_f, bm_f, cm_f, dtc_f, dtt_f, alc_ref[0], alr_ref[0], yf_ref, state_ref.at[0])
    _ssd_chunk(False, xs_b, bm_b, cm_b, dtc_b, dtt_b, alc_ref[1], alr_ref[1], yb_ref, state_ref.at[1])


def _ssd_chunk_index(b, d, k):
    lat_chunks = SEQ // CHUNK
    ctx_chunks = CTX // CHUNK
    pos = jnp.where(d == 0, k,
                    jnp.where(k < ctx_chunks, ctx_chunks - 1 - k, 2 * ctx_chunks + lat_chunks - 1 - k))
    return jnp.where(pos < ctx_chunks,
                     T_LAT // CHUNK + ctx_chunks * b + pos,
                     lat_chunks * b + pos - ctx_chunks)


def _ssd(xs, bm, cm, dtc, dtt, a_log):
    n = xs.shape[0]
    steps = (SEQ + CTX) // CHUNK
    cidx = _ssd_chunk_index

    def dir_specs(d):
        blk = lambda b, k: (cidx(b, d, k), 0)
        return [
            pl.BlockSpec((CHUNK, D), blk),
            pl.BlockSpec((CHUNK, 2 * SSD_STATE), blk),
            pl.BlockSpec((CHUNK, 2 * SSD_STATE), blk),
            pl.BlockSpec((1, CHUNK, HEADS), lambda b, k: (d, cidx(b, d, k), 0)),
            pl.BlockSpec((HEADS, CHUNK), lambda b, k: (d, cidx(b, d, k))),
        ]

    y = jax.ShapeDtypeStruct((n, D), BF16)
    return pl.pallas_call(
        _ssd_kernel,
        grid=(BATCH, steps),
        in_specs=dir_specs(0) + dir_specs(1) + [_const_spec((2, HEADS, 1)), _const_spec((2, 1, HEADS))],
        out_specs=[pl.BlockSpec((CHUNK, D), lambda b, k: (cidx(b, 0, k), 0)),
                   pl.BlockSpec((CHUNK, D), lambda b, k: (cidx(b, 1, k), 0))],
        out_shape=[y, y],
        scratch_shapes=[pltpu.VMEM((2, SSD_STATE, D), F32)],
        compiler_params=_cparams(2),
        name="ssd_scan",
    )(xs, bm, cm, dtc, dtt, xs, bm, cm, dtc, dtt,
      a_log.reshape(2, HEADS, 1), a_log.reshape(2, 1, HEADS))


def _window_sum(e, w):
    acc = e
    span = 1
    while span < w:
        n = acc.shape[0] - span
        acc = acc[:n] + acc[span:span + n]
        span *= 2
    start = HALO - w // 2
    return acc[start:start + SEQ_TILE]


def _finish_kernel(x_ref, mod_ref, yf_ref, yb_ref, xs_ref, z_ref, uprev_ref, up_ref, unext_ref,
                   dsk_ref, gn_ref, pw_ref, ps_ref, ow_ref, o_ref):
    first, last, row0, seq_len = _seq_tile_info(pl.program_id(0))
    y = yf_ref[...].astype(F32) + yb_ref[...].astype(F32) + dsk_ref[...] * xs_ref[...].astype(F32)
    y = y * _silu(z_ref[...].astype(F32))
    ms = jnp.mean(y * y, axis=-1, keepdims=True)
    yn = (y * lax.rsqrt(ms + EPS) * gn_ref[...]).astype(BF16)
    out = _dot(yn, ow_ref[0:D, :])
    up = up_ref[...].astype(F32)
    prev = jnp.where(first, 0.0, uprev_ref[...].astype(F32))
    nxt = jnp.where(last, 0.0, unext_ref[...].astype(F32))
    ext = jnp.concatenate([prev, up, nxt], axis=0)
    pos = row0 + lax.broadcasted_iota(jnp.int32, (SEQ_TILE, 128), 0)
    for g, w in enumerate(POOL_WINDOWS):
        sl = slice(g * POOL_GROUP, (g + 1) * POOL_GROUP)
        lo = jnp.maximum(pos - w // 2, 0)
        hi = jnp.minimum(pos + (w - 1 - w // 2), seq_len - 1)
        inv_cnt = 1.0 / (hi - lo + 1).astype(F32)
        mean = _window_sum(ext[:, sl], w) * jnp.concatenate([inv_cnt] * (POOL_GROUP // 128), axis=1)
        m = (mean - up[:, sl]).astype(BF16)
        pooled = (_dot(m, pw_ref[g]) * ps_ref[:, sl]).astype(BF16)
        out = out + _dot(pooled, ow_ref[D + g * POOL_GROUP:D + (g + 1) * POOL_GROUP, :])
    o_ref[...] = x_ref[...] + mod_ref[0, 5:6, :] * out


def _finish(x, mod, yf, yb, xs, z, up, d_skip, gn_g, pool_w, pool_scale, out_w):
    n = x.shape[0]
    prev, nxt = _halo_specs(D, HALO)
    row = lambda i: (i, 0)
    tile = pl.BlockSpec((SEQ_TILE, D), row)
    return pl.pallas_call(
        _finish_kernel,
        grid=(n // SEQ_TILE,),
        in_specs=[
            tile,
            pl.BlockSpec((1, N_MOD, D), lambda i, f=_mod_row(SEQ_TILE): (f(i), 0, 0)),
            tile, tile, tile, tile, prev, tile, nxt,
            _const_spec((1, D)), _const_spec((1, D)),
            _const_spec((N_POOL, POOL_GROUP, POOL_GROUP)), _const_spec((1, D)),
            _const_spec((2 * D, D)),
        ],
        out_specs=tile,
        out_shape=jax.ShapeDtypeStruct((n, D), F32),
        compiler_params=_cparams(1),
        name="ssd_finish",
    )(x, mod, yf, yb, xs, z, up, up, up, d_skip, gn_g.reshape(1, D), pool_w,
      pool_scale.reshape(1, D), out_w)


def _qkv_kernel(x_ref, mod_ref, g_ref, w_ref, q_ref, k_ref, v_ref):
    h = _rms_mod(x_ref[...], g_ref[...], mod_ref[0, 3:4, :], mod_ref[0, 4:5, :]).astype(BF16)
    t = _dot(h, w_ref[...])
    q_ref[...] = (t[:, :D] * (HEAD_DIM ** -0.5)).astype(BF16)
    k_ref[...] = t[:, D:2 * D].astype(BF16)
    v_ref[...] = t[:, 2 * D:].astype(BF16)


def _qkv(x, mod, g, w, tm=512):
    n = x.shape[0]
    row = lambda i: (i, 0)
    out = jax.ShapeDtypeStruct((n, D), BF16)
    return pl.pallas_call(
        _qkv_kernel,
        grid=(n // tm,),
        in_specs=[pl.BlockSpec((tm, D), row),
                  pl.BlockSpec((1, N_MOD, D), lambda i, f=_mod_row(tm): (f(i), 0, 0)),
                  _const_spec((1, D)), _const_spec((D, 3 * D))],
        out_specs=[pl.BlockSpec((tm, D), row)] * 3,
        out_shape=[out, out, out],
        compiler_params=_cparams(1),
        name="na_qkv",
    )(x, mod, g.reshape(1, D), w)


NA_QROWS = 4
NA_GROUPS = GRID_ROWS // NA_QROWS
NA_KTILES = 3
NA_QTOK = NA_QROWS * GRID_W
NA_KTOK = NA_KTILES * NA_QROWS * GRID_W
NA_KINDS = 3


def _na_first_tile(m):
    return jnp.clip(m - 1, 0, NA_GROUPS - NA_KTILES)


def _na_build_bias(rpb_ref, bias_ref):
    shape = (GRID_W, 2 * GRID_W)
    jq = lax.broadcasted_iota(jnp.int32, shape, 0)
    ln = lax.broadcasted_iota(jnp.int32, shape, 1)
    low = ln < GRID_W
    kcol = jnp.where(low, ln, ln - GRID_W)
    cs = jnp.clip(jq - WIN_COLS // 2, 0, GRID_W - WIN_COLS)
    col_ok = (kcol >= cs) & (kcol < cs + WIN_COLS)
    neg = jnp.full(shape, -jnp.inf, F32)
    for hh in range(2):
        lo_blk, hi_blk = {}, {}
        for dr in range(-(WIN_ROWS - 1), WIN_ROWS):
            row = jnp.broadcast_to(rpb_ref[hh, dr + WIN_ROWS - 1:dr + WIN_ROWS, :], shape)
            lo_blk[dr] = pltpu.roll(row, GRID_W + 1, 1, stride=1, stride_axis=0)
            hi_blk[dr] = pltpu.roll(row, 1, 1, stride=1, stride_axis=0)
        for kind, m in enumerate((0, NA_GROUPS // 2, NA_GROUPS - 1)):
            t0 = int(np.clip(m - 1, 0, NA_GROUPS - NA_KTILES))
            for qi in range(NA_QROWS):
                i = m * NA_QROWS + qi
                rs = int(np.clip(i - WIN_ROWS // 2, 0, GRID_ROWS - WIN_ROWS))
                r0 = hh * NA_QTOK + qi * GRID_W
                for p in range(NA_KTILES * NA_QROWS // 2):
                    kr = t0 * NA_QROWS + 2 * p
                    a = lo_blk[kr - i] if rs <= kr < rs + WIN_ROWS else neg
                    b = hi_blk[kr + 1 - i] if rs <= kr + 1 < rs + WIN_ROWS else neg
                    tile = jnp.where(col_ok, jnp.where(low, a, b), neg)
                    bias_ref[kind, r0:r0 + GRID_W, p * 2 * GRID_W:(p + 1) * 2 * GRID_W] = tile


def _na_kernel(q_ref, k_ref, v_ref, kc_ref, vc_ref, rpb_ref, o_ref, bias_ref):
    @pl.when(pl.program_id(1) == 0)
    def _():
        _na_build_bias(rpb_ref, bias_ref)

    lane = lax.broadcasted_iota(jnp.int32, (NA_QTOK, 2 * HEAD_DIM), 1)
    low_half = lane < HEAD_DIM
    kc = kc_ref[...]
    vc = vc_ref[...]

    def group(m, carry):
        kind = jnp.where(m == 0, 0, jnp.where(m == NA_GROUPS - 1, 2, 1))
        q0 = pl.multiple_of(m * NA_QTOK, NA_QTOK)
        k0 = pl.multiple_of(_na_first_tile(m) * NA_QTOK, NA_QTOK)
        q4 = q_ref[pl.ds(q0, NA_QTOK), :]
        zero = jnp.zeros_like(q4)
        qs = jnp.concatenate([jnp.where(low_half, q4, zero), jnp.where(low_half, zero, q4)], axis=0)
        k_win = k_ref[pl.ds(k0, NA_KTOK), :]
        v_win = v_ref[pl.ds(k0, NA_KTOK), :]
        s_win = _dot_nt(qs, k_win) + bias_ref[kind]
        s_ctx = _dot_nt(qs, kc)
        mx = jnp.maximum(jnp.max(s_win, axis=1, keepdims=True),
                         jnp.max(s_ctx, axis=1, keepdims=True))
        p_win = jnp.exp(s_win - mx)
        p_ctx = jnp.exp(s_ctx - mx)
        den = jnp.sum(p_win, axis=1, keepdims=True) + jnp.sum(p_ctx, axis=1, keepdims=True)
        o = (_dot(p_win.astype(BF16), v_win) + _dot(p_ctx.astype(BF16), vc)) / den
        o_ref[pl.ds(q0, NA_QTOK), :] = jnp.where(low_half, o[:NA_QTOK], o[NA_QTOK:]).astype(BF16)
        return carry

    lax.fori_loop(0, NA_GROUPS, group, 0, unroll=8)


def _na(q, k, v, rpb):
    pair = 2 * HEAD_DIM
    ctx_blk0 = T_LAT // CTX
    seq = pl.BlockSpec((SEQ, pair), lambda hp, b: (b, hp))
    ctx = pl.BlockSpec((CTX, pair), lambda hp, b: (ctx_blk0 + b, hp))
    left = GRID_W - WIN_COLS
    rpb_pad = jnp.pad(rpb.astype(F32), ((0, 0), (0, 0), (left, 2 * GRID_W - rpb.shape[2] - left)))
    return pl.pallas_call(
        _na_kernel,
        grid=(HEADS // 2, BATCH),
        in_specs=[seq, seq, seq, ctx, ctx,
                  pl.BlockSpec((2, 2 * WIN_ROWS - 1, 2 * GRID_W), lambda hp, b: (hp, 0, 0))],
        out_specs=seq,
        out_shape=jax.ShapeDtypeStruct((T_LAT, D), BF16),
        scratch_shapes=[pltpu.VMEM((NA_KINDS, 2 * NA_QTOK, NA_KTOK), F32)],
        compiler_params=_cparams(2),
        name="na_attn",
    )(q, k, v, k, v, rpb_pad)


def kernel(x, c, ctx, c_ctx, ada_w, ada_b, norm_g, ffn_w1, ffn_w3, ffn_w2, ssd_in_w, ssd_conv_w,
           ssd_conv_b, ssd_dt_bias, ssd_a_log, ssd_d, ssd_norm_g, pool_w, pool_scale, mix_a_out_w,
           na_qkv_w, na_rpb, na_out_w, final_norm_g):
    c_all = jnp.concatenate(
        [c, c_ctx[None, :], jnp.zeros((MOD_ROWS - BATCH - 1, D), F32)], axis=0)
    mod = _ada(c_all, ada_w, ada_b).reshape(2, MOD_ROWS, N_MOD, D)
    w1 = ffn_w1.astype(BF16)
    w3 = ffn_w3.astype(BF16)
    w2 = ffn_w2.astype(BF16)

    m0 = mod[0]
    xf = _ffn(x.reshape(T_LAT, D), m0, 0, norm_g[0, 0], w1[0, 0], w3[0, 0], w2[0, 0], n_rows=N_TOK,
              x_ctx=ctx.reshape(T_CTX, D))
    in_w = ssd_in_w[0]
    o1, o2, o3 = D, D + CONV_DIM, D + CONV_DIM + 2 * HEADS
    w_dt = in_w[:, o2:o3]
    z, xs, bm, cm, up, dtc, dtt = _inproj(
        xf, m0, norm_g[0, 1],
        in_w[:, :o1].astype(BF16), in_w[:, o1:o2].astype(BF16), in_w[:, o3:].astype(BF16),
        w_dt[:, :HEADS].astype(BF16), w_dt[:, HEADS:].astype(BF16), w_dt.T.astype(BF16),
        ssd_dt_bias[0].reshape(2, 1, HEADS), ssd_dt_bias[0].reshape(2 * HEADS, 1),
        ssd_conv_w[0], ssd_conv_b[0])
    yf, yb = _ssd(xs, bm, cm, dtc, dtt, ssd_a_log[0])
    d_skip = jnp.repeat(ssd_d[0].astype(F32), HEAD_DIM).reshape(1, D)
    xf = _finish(xf, m0, yf, yb, xs, z, up, d_skip, ssd_norm_g[0], pool_w[0].astype(BF16),
                 pool_scale[0], mix_a_out_w[0].astype(BF16))
    xf = _ffn(xf, m0, 6, norm_g[0, 2], w1[0, 1], w3[0, 1], w2[0, 1], n_rows=N_TOK)

    m1 = mod[1]
    xf = _ffn(xf, m1, 0, norm_g[1, 0], w1[1, 0], w3[1, 0], w2[1, 0], n_rows=N_TOK)
    q, k, v = _qkv(xf, m1, norm_g[1, 1], na_qkv_w[0].astype(BF16))
    att = _na(q, k, v, na_rpb[0])
    xl = _ffn(xf, m1, 6, norm_g[1, 2], w1[1, 1], w3[1, 1], w2[1, 1], n_rows=T_LAT,
              final_g=final_norm_g, mixer_out=(att, na_out_w[0].astype(BF16)))
    return xl.reshape(BATCH, SEQ, D)
```
